```python
import math
import jax, jax.numpy as jnp
from jax import lax
import numpy as np

D_MODEL = 1024
BATCH = 8
SEQ = 2048
DEPTH = 4
DEC_BATCH = 128
DEC_SEQ = 8
PAST_LEN = 16384
PAGE_SIZE = 128

SSD_D_INNER = D_MODEL
SSD_HEADDIM = 64
SSD_HEADS = SSD_D_INNER // SSD_HEADDIM
SSD_D_STATE = 64
SSD_GROUPS = 4
SSD_CONV = 4
SSD_CHUNK = 128
SSD_CONV_CH = SSD_D_INNER + 2 * SSD_GROUPS * SSD_D_STATE
S5_WIDTH = D_MODEL // 2
S5_GROUP = 16
S5_GROUPS = S5_WIDTH // S5_GROUP
S5_STATE = 64
HG_WIDTH = D_MODEL // 2
HG_HEADS = 4
HG_HEADDIM = HG_WIDTH // HG_HEADS
HG_CHUNK = 64
RW_WIDTH = D_MODEL // 2
RW_HEADDIM = 64
RW_HEADS = RW_WIDTH // RW_HEADDIM
RW_W_RANK = 64
RW_A_RANK = 64
RW_G_RANK = 128
RW_PROJ = 3 * RW_WIDTH + RW_W_RANK + RW_A_RANK + RW_G_RANK
RW_LN_EPS = 64e-5
N_BRANCH = 4
D_FF = 2816
FFN_CONV = 3
EPS = 1e-6

IN_SIZES = (SSD_D_INNER, SSD_CONV_CH, SSD_HEADS, S5_WIDTH, HG_WIDTH, HG_WIDTH, HG_WIDTH, HG_WIDTH, RW_PROJ)
IN_WIDTH = sum(IN_SIZES)
IN_OFFSETS = tuple(int(o) for o in np.cumsum(IN_SIZES)[:-1])
RW_OFFSETS = (RW_WIDTH, 2 * RW_WIDTH, 3 * RW_WIDTH, 3 * RW_WIDTH + RW_W_RANK, 3 * RW_WIDTH + RW_W_RANK + RW_A_RANK)

kernel_name = 'hybrid_ssd_s5_hgrn2_rwkv7_step'

F32 = jnp.float32


def rmsnorm(x, w):
    xf = x.astype(F32)
    xf = xf * lax.rsqrt(jnp.mean(xf * xf, axis=-1, keepdims=True) + EPS)
    return (xf * w.astype(F32)).astype(x.dtype)


def causal_dwconv(u, buf, w, b):
    k = w.shape[0]
    n = u.shape[1]
    full = jnp.concatenate([buf.astype(u.dtype), u], axis=1)
    out = b
    for j in range(k):
        out = out + full[:, j:j + n] * w[j]
    return out, full[:, n:]


def pad_len(a, pad):
    return jnp.pad(a, [(0, 0), (0, pad)] + [(0, 0)] * (a.ndim - 2))


def segsum(a):
    t = a.shape[-1]
    ae = jnp.broadcast_to(a[..., None], a.shape + (t,))
    ae = jnp.where(jnp.tril(jnp.ones((t, t), bool), -1), ae, 0.0)
    ss = jnp.cumsum(ae, axis=-2)
    return jnp.where(jnp.tril(jnp.ones((t, t), bool), 0), ss, -jnp.inf)


def ssd_scan(xdt, dta, bm, cm, h0):
    bsz, n, h, pdim = xdt.shape
    c = min(SSD_CHUNK, n)
    pad = (-n) % c
    xdt, dta, bm, cm = (pad_len(t, pad) for t in (xdt, dta, bm, cm))
    nc = (n + pad) // c
    xc = xdt.reshape(bsz, nc, c, h, pdim)
    bc = bm.reshape(bsz, nc, c, h, SSD_D_STATE)
    cc = cm.reshape(bsz, nc, c, h, SSD_D_STATE)
    a = dta.reshape(bsz, nc, c, h).transpose(0, 3, 1, 2)
    a_cum = jnp.cumsum(a, axis=-1)
    lmat = jnp.exp(segsum(a))
    y_diag = jnp.einsum('bclhn,bcshn,bhcls,bcshp->bclhp', cc, bc, lmat, xc)
    decay_states = jnp.exp(a_cum[..., -1:] - a_cum)
    states = jnp.einsum('bclhn,bhcl,bclhp->bchpn', bc, decay_states, xc)
    states = jnp.concatenate([h0[:, None], states], axis=1)
    chunk_tot = jnp.pad(a_cum[..., -1], ((0, 0), (0, 0), (1, 0)))
    decay_chunk = jnp.exp(segsum(chunk_tot))
    states = jnp.einsum('bhzc,bchpn->bzhpn', decay_chunk, states)
    y_off = jnp.einsum('bclhn,bchpn,bhcl->bclhp', cc, states[:, :-1], jnp.exp(a_cum))
    y = (y_diag + y_off).reshape(bsz, nc * c, h, pdim)[:, :n]
    return y, states[:, -1]


def ssd_mixer(z, xbc, dt_raw, h0, conv_buf, p):
    bsz, n, _ = z.shape
    xbc, new_buf = causal_dwconv(xbc, conv_buf, p['ssd_conv_w'], p['ssd_conv_b'])
    xbc = jax.nn.silu(xbc.astype(F32))
    xs, bm, cm = jnp.split(xbc, [SSD_D_INNER, SSD_D_INNER + SSD_GROUPS * SSD_D_STATE], axis=-1)
    rep = SSD_HEADS // SSD_GROUPS
    xs = xs.reshape(bsz, n, SSD_HEADS, SSD_HEADDIM)
    bm = jnp.repeat(bm.reshape(bsz, n, SSD_GROUPS, SSD_D_STATE), rep, axis=2)
    cm = jnp.repeat(cm.reshape(bsz, n, SSD_GROUPS, SSD_D_STATE), rep, axis=2)
    dt = jax.nn.softplus(dt_raw.astype(F32) + p['ssd_dt_bias'].astype(F32))
    a = -jnp.exp(p['ssd_a_log'].astype(F32))
    y, h_new = ssd_scan(xs * dt[..., None], dt * a, bm, cm, h0.astype(F32))
    y = y + xs * p['ssd_d'].astype(F32)[:, None]
    y = y.reshape(bsz, n, SSD_D_INNER)
    y = rmsnorm(y * jax.nn.silu(z.astype(F32)), p['ssd_norm_w'])
    return y.astype(z.dtype), h_new, new_buf


def s5_combine(e1, e2):
    a1r, a1i, b1r, b1i = e1
    a2r, a2i, b2r, b2i = e2
    return (a2r * a1r - a2i * a1i, a2r * a1i + a2i * a1r,
            a2r * b1r - a2i * b1i + b2r, a2r * b1i + a2i * b1r + b2i)


def s5_mixer(u, h0_re, h0_im, p):
    bsz, n, _ = u.shape
    uf = u.astype(F32)
    ug = uf.reshape(bsz, n, S5_GROUPS, S5_GROUP)
    dt = jnp.exp(p['s5_log_dt'].astype(F32))[:, None]
    a_re = p['s5_a_re'].astype(F32)
    a_im = p['s5_a_im'].astype(F32)
    mag = jnp.exp(dt * a_re)
    ab_re = mag * jnp.cos(dt * a_im)
    ab_im = mag * jnp.sin(dt * a_im)
    den = a_re * a_re + a_im * a_im
    q_re = ((ab_re - 1.0) * a_re + ab_im * a_im) / den
    q_im = (ab_im * a_re - (ab_re - 1.0) * a_im) / den
    b_re = p['s5_b_re'].astype(F32)
    b_im = p['s5_b_im'].astype(F32)
    bb_re = q_re[..., None] * b_re - q_im[..., None] * b_im
    bb_im = q_re[..., None] * b_im + q_im[..., None] * b_re
    bu_re = jnp.einsum('gnj,blgj->blgn', bb_re, ug)
    bu_im = jnp.einsum('gnj,blgj->blgn', bb_im, ug)
    a_seq_re = jnp.broadcast_to(ab_re, bu_re.shape)
    a_seq_im = jnp.broadcast_to(ab_im, bu_im.shape)
    cr, ci, hr, hi = lax.associative_scan(s5_combine, (a_seq_re, a_seq_im, bu_re, bu_im), axis=1)
    h0r = h0_re.astype(F32)[:, None]
    h0i = h0_im.astype(F32)[:, None]
    hr = hr + cr * h0r - ci * h0i
    hi = hi + cr * h0i + ci * h0r
    y = (jnp.einsum('gjn,blgn->blgj', p['s5_c_re'].astype(F32), hr)
         - jnp.einsum('gjn,blgn->blgj', p['s5_c_im'].astype(F32), hi))
    y = y.reshape(bsz, n, S5_WIDTH) + p['s5_d'].astype(F32) * uf
    y = jax.nn.gelu(y)
    y = y * jax.nn.sigmoid(y @ p['s5_glu_w'].astype(F32) + p['s5_glu_b'].astype(F32))
    return y.astype(u.dtype), hr[:, -1], hi[:, -1]


def hgrn_chunked(q, k, v, logf, s0):
    bsz, n, h, _ = q.shape
    c = min(HG_CHUNK, n)
    pad = (-n) % c
    nc = (n + pad) // c

    def blocks(t):
        t = pad_len(t, pad).reshape(bsz, nc, c, h, t.shape[-1])
        return t.transpose(1, 0, 3, 2, 4)

    causal = jnp.tril(jnp.ones((c, c), bool))

    def step(s, blk):
        qc, kc, vc, gc = blk
        b = jnp.cumsum(gc, axis=-2)
        o_inter = jnp.einsum('bhtk,bhkv->bhtv', qc * jnp.exp(b), s)
        diff = b[..., :, None, :] - b[..., None, :, :]
        decay = jnp.exp(jnp.where(causal[:, :, None], diff, -jnp.inf))
        att = jnp.einsum('bhtk,bhsk,bhtsk->bhts', qc, kc, decay)
        o_intra = jnp.einsum('bhts,bhsv->bhtv', att, vc)
        b_last = b[..., -1:, :]
        s = (jnp.exp(b_last[..., 0, :])[..., None] * s
             + jnp.einsum('bhsk,bhsv->bhkv', kc * jnp.exp(b_last - b), vc))
        return s, o_inter + o_intra

    s, o = lax.scan(step, s0, tuple(blocks(t) for t in (q, k, v, logf)))
    o = o.transpose(1, 0, 3, 2, 4).reshape(bsz, nc * c, h, -1)[:, :n]
    return o, s


def hgrn_mixer(q, f_raw, i_in, g, s0, lb, p):
    bsz, n, _ = q.shape
    shp = (bsz, n, HG_HEADS, HG_HEADDIM)
    lbf = lb.astype(F32)
    f = lbf + (1.0 - lbf) * jax.nn.sigmoid(f_raw.astype(F32))
    logf = jnp.log(f).reshape(shp)
    k = (1.0 - f).reshape(shp)
    qf = jax.nn.silu(q.astype(F32)).reshape(shp)
    v = i_in.astype(F32).reshape(shp)
    o, s = hgrn_chunked(qf, k, v, logf, s0.astype(F32))
    o = rmsnorm(o, p['hg_norm_w']) * jax.nn.sigmoid(g.astype(F32)).reshape(shp)
    return o.reshape(bsz, n, HG_WIDTH).astype(q.dtype), s


def rwkv_scan(r, w, k, v, a, b, s0):
    def step(s, inp):
        rt, wt, kt, vt, at, bt = inp
        sa = jnp.einsum('bhvk,bhk->bhv', s, at)
        s = s * wt[:, :, None, :] + sa[..., None] * bt[:, :, None, :] + vt[..., None] * kt[:, :, None, :]
        return s, jnp.einsum('bhvk,bhk->bhv', s, rt)

    seq = tuple(jnp.swapaxes(t, 0, 1) for t in (r, w, k, v, a, b))
    s, y = lax.scan(step, s0, seq)
    return jnp.swapaxes(y, 0, 1), s


def rwkv_mixer(pr, shift0, s0, p):
    bsz, n, _ = pr.shape
    prev = jnp.concatenate([shift0[:, None].astype(pr.dtype), pr[:, :-1]], axis=1)
    xm = (pr + (prev - pr) * p['rw_mu']).astype(F32)
    r, k, v, wd, ad, gd = jnp.split(xm, RW_OFFSETS, axis=-1)
    w = -jax.nn.softplus(-(p['rw_w0'].astype(F32) + jnp.tanh(wd) @ p['rw_w_up'].astype(F32))) - 0.5
    decay = jnp.exp(-jnp.exp(w))
    a = jax.nn.sigmoid(p['rw_a0'].astype(F32) + ad @ p['rw_a_up'].astype(F32))
    g = jax.nn.sigmoid(gd) @ p['rw_g_up'].astype(F32)
    shp = (bsz, n, RW_HEADS, RW_HEADDIM)
    r, k, v, decay, a = (t.reshape(shp) for t in (r, k, v, decay, a))
    kk = k * p['rw_k_k'].astype(F32)
    kk = kk * lax.rsqrt(jnp.maximum(jnp.sum(kk * kk, axis=-1, keepdims=True), 1e-24))
    k = k * (1.0 + (a - 1.0) * p['rw_k_a'].astype(F32))
    y, s = rwkv_scan(r, decay, k, v, -kk, kk * a, s0.astype(F32))
    mu = jnp.mean(y, axis=-1, keepdims=True)
    var = jnp.mean(jnp.square(y - mu), axis=-1, keepdims=True)
    y = (y - mu) * lax.rsqrt(var + RW_LN_EPS) * p['rw_ln_w'].astype(F32) + p['rw_ln_b'].astype(F32)
    y = y + jnp.sum(r * k * p['rw_r_k'].astype(F32), axis=-1, keepdims=True) * v
    y = y.reshape(bsz, n, RW_WIDTH) * g
    return y.astype(pr.dtype), s, pr[:, -1]


def block(x, st, p, lb):
    h_ssd, buf_ssd, s5r, s5i, s_hg, s_rw, sh_rw, buf_ffn = st
    bsz, n, _ = x.shape
    xn = rmsnorm(x, p['norm1_w'])
    proj = xn @ p['w_in']
    z, xbc, dt_raw, u_s5, q_hg, f_hg, i_hg, g_hg, p_rw = jnp.split(proj, IN_OFFSETS, axis=-1)
    y_ssd, h_ssd, buf_ssd = ssd_mixer(z, xbc, dt_raw, h_ssd, buf_ssd, p)
    y_s5, s5r, s5i = s5_mixer(u_s5, s5r, s5i, p)
    y_hg, s_hg = hgrn_mixer(q_hg, f_hg, i_hg, g_hg, s_hg, lb, p)
    y_rw, s_rw, sh_rw = rwkv_mixer(p_rw, sh_rw, s_rw, p)
    gates = jax.nn.sigmoid((xn @ p['w_merge'] + p['b_merge']).astype(F32)).astype(x.dtype)
    gates = gates.reshape(bsz, n, N_BRANCH, D_MODEL)
    merged = (gates[:, :, 0] * (y_ssd @ p['w_br_ssd'])
              + gates[:, :, 1] * (y_s5 @ p['w_br_s5'])
              + gates[:, :, 2] * (y_hg @ p['w_br_hg'])
              + gates[:, :, 3] * (y_rw @ p['w_br_rw']))
    x = x + merged @ p['w_out']
    xn2 = rmsnorm(x, p['norm2_w'])
    up, buf_ffn = causal_dwconv(xn2 @ p['ffn_up'], buf_ffn, p['ffn_conv_w'], p['ffn_conv_b'])
    gate_h, val_h = jnp.split(up, 2, axis=-1)
    x = x + (jax.nn.gelu(gate_h) * val_h) @ p['ffn_down']
    new_st = (h_ssd, buf_ssd, s5r, s5i, s_hg, s_rw, sh_rw, buf_ffn)
    return x, tuple(s.astype(x.dtype) for s in new_st)


def zero_states(bsz, dtype):
    return (jnp.zeros((bsz, SSD_HEADS, SSD_HEADDIM, SSD_D_STATE), dtype),
            jnp.zeros((bsz, SSD_CONV - 1, SSD_CONV_CH), dtype),
            jnp.zeros((bsz, S5_GROUPS, S5_STATE), dtype),
            jnp.zeros((bsz, S5_GROUPS, S5_STATE), dtype),
            jnp.zeros((bsz, HG_HEADS, HG_HEADDIM, HG_HEADDIM), dtype),
            jnp.zeros((bsz, RW_HEADS, RW_HEADDIM, RW_HEADDIM), dtype),
            jnp.zeros((bsz, RW_PROJ), dtype),
            jnp.zeros((bsz, FFN_CONV - 1, 2 * D_FF), dtype))


def trunk(x, init_states, params, lb_all, final_norm_w):
    new_states = []
    for l in range(DEPTH):
        p = {name: arr[l] for name, arr in params.items()}
        x, st = block(x, init_states[l], p, lb_all[l])
        new_states.append(st)
    stacked = tuple(jnp.stack([st[i] for st in new_states], axis=0) for i in range(len(new_states[0])))
    return rmsnorm(x, final_norm_w), stacked


def setup_inputs(seed: int = 0) -> dict:
    key = jax.random.key(seed)
    ks = iter(jax.random.split(key, 128))

    def nrm(shape, scale):
        return scale * jax.random.normal(next(ks), shape, F32)

    def unif(shape, lo, hi):
        return jax.random.uniform(next(ks), shape, F32, lo, hi)

    L = DEPTH
    dt0 = jnp.exp(unif((L, SSD_HEADS), math.log(1e-3), math.log(1e-1)))
    inp = {}
    inp['x_prompt'] = nrm((BATCH, SEQ, D_MODEL), 1.0)
    inp['x_sample'] = nrm((DEC_BATCH, DEC_SEQ, D_MODEL), 1.0)
    inp['state_ssd'] = nrm((L, DEC_BATCH, SSD_HEADS, SSD_HEADDIM, SSD_D_STATE), 0.5)
    inp['state_ssd_conv'] = nrm((L, DEC_BATCH, SSD_CONV - 1, SSD_CONV_CH), 1.0)
    inp['state_s5_re'] = nrm((L, DEC_BATCH, S5_GROUPS, S5_STATE), 0.1)
    inp['state_s5_im'] = nrm((L, DEC_BATCH, S5_GROUPS, S5_STATE), 0.1)
    inp['state_hgrn'] = nrm((L, DEC_BATCH, HG_HEADS, HG_HEADDIM, HG_HEADDIM), 0.5)
    inp['state_rwkv'] = nrm((L, DEC_BATCH, RW_HEADS, RW_HEADDIM, RW_HEADDIM), 0.3)
    inp['state_rwkv_shift'] = nrm((L, DEC_BATCH, RW_PROJ), 1.0)
    inp['state_ffn_conv'] = nrm((L, DEC_BATCH, FFN_CONV - 1, 2 * D_FF), 1.0)
    inp['norm1_w'] = 1.0 + nrm((L, D_MODEL), 0.02)
    inp['w_in'] = nrm((L, D_MODEL, IN_WIDTH), D_MODEL ** -0.5)
    inp['ssd_conv_w'] = nrm((L, SSD_CONV, SSD_CONV_CH), SSD_CONV ** -0.5)
    inp['ssd_conv_b'] = nrm((L, SSD_CONV_CH), 0.02)
    inp['ssd_dt_bias'] = dt0 + jnp.log(-jnp.expm1(-dt0))
    inp['ssd_a_log'] = jnp.log(unif((L, SSD_HEADS), 1.0, 16.0))
    inp['ssd_d'] = 1.0 + nrm((L, SSD_HEADS), 0.1)
    inp['ssd_norm_w'] = 1.0 + nrm((L, SSD_D_INNER), 0.02)
    inp['s5_a_re'] = -0.5 + nrm((L, S5_GROUPS, S5_STATE), 0.01)
    inp['s5_a_im'] = (jnp.broadcast_to(math.pi * jnp.arange(S5_STATE, dtype=F32), (L, S5_GROUPS, S5_STATE))
                      + nrm((L, S5_GROUPS, S5_STATE), 0.01))
    inp['s5_log_dt'] = unif((L, S5_GROUPS), math.log(1e-3), math.log(1e-1))
    inp['s5_b_re'] = nrm((L, S5_GROUPS, S5_STATE, S5_GROUP), (2.0 * S5_GROUP) ** -0.5)
    inp['s5_b_im'] = nrm((L, S5_GROUPS, S5_STATE, S5_GROUP), (2.0 * S5_GROUP) ** -0.5)
    inp['s5_c_re'] = nrm((L, S5_GROUPS, S5_GROUP, S5_STATE), (2.0 * S5_STATE) ** -0.5)
    inp['s5_c_im'] = nrm((L, S5_GROUPS, S5_GROUP, S5_STATE), (2.0 * S5_STATE) ** -0.5)
    inp['s5_d'] = nrm((L, S5_WIDTH), 1.0)
    inp['s5_glu_w'] = nrm((L, S5_WIDTH, S5_WIDTH), S5_WIDTH ** -0.5)
    inp['s5_glu_b'] = nrm((L, S5_WIDTH), 0.02)
    inp['hg_lb_raw'] = nrm((L, HG_WIDTH), 0.5)
    inp['hg_norm_w'] = 1.0 + nrm((L, HG_HEADDIM), 0.02)
    inp['rw_mu'] = unif((L, RW_PROJ), 0.0, 1.0)
    inp['rw_w0'] = unif((L, RW_WIDTH), -4.0, 0.0)
    inp['rw_w_up'] = nrm((L, RW_W_RANK, RW_WIDTH), 0.3 * RW_W_RANK ** -0.5)
    inp['rw_a0'] = nrm((L, RW_WIDTH), 0.1)
    inp['rw_a_up'] = nrm((L, RW_A_RANK, RW_WIDTH), 0.5 * RW_A_RANK ** -0.5)
    inp['rw_g_up'] = nrm((L, RW_G_RANK, RW_WIDTH), RW_G_RANK ** -0.5)
    inp['rw_k_k'] = 0.85 + nrm((L, RW_HEADS, RW_HEADDIM), 0.05)
    inp['rw_k_a'] = 1.0 + nrm((L, RW_HEADS, RW_HEADDIM), 0.05)
    inp['rw_r_k'] = nrm((L, RW_HEADS, RW_HEADDIM), 0.1)
    inp['rw_ln_w'] = 1.0 + nrm((L, RW_HEADS, RW_HEADDIM), 0.02)
    inp['rw_ln_b'] = nrm((L, RW_HEADS, RW_HEADDIM), 0.02)
    inp['w_br_ssd'] = nrm((L, SSD_D_INNER, D_MODEL), SSD_D_INNER ** -0.5)
    inp['w_br_s5'] = nrm((L, S5_WIDTH, D_MODEL), S5_WIDTH ** -0.5)
    inp['w_br_hg'] = nrm((L, HG_WIDTH, D_MODEL), HG_WIDTH ** -0.5)
    inp['w_br_rw'] = nrm((L, RW_WIDTH, D_MODEL), RW_WIDTH ** -0.5)
    inp['w_merge'] = nrm((L, D_MODEL, N_BRANCH * D_MODEL), D_MODEL ** -0.5)
    inp['b_merge'] = nrm((L, N_BRANCH * D_MODEL), 0.02)
    inp['w_out'] = nrm((L, D_MODEL, D_MODEL), 0.5 * D_MODEL ** -0.5)
    inp['norm2_w'] = 1.0 + nrm((L, D_MODEL), 0.02)
    inp['ffn_up'] = nrm((L, D_MODEL, 2 * D_FF), D_MODEL ** -0.5)
    inp['ffn_conv_w'] = nrm((L, FFN_CONV, 2 * D_FF), FFN_CONV ** -0.5)
    inp['ffn_conv_b'] = nrm((L, 2 * D_FF), 0.02)
    inp['ffn_down'] = nrm((L, D_FF, D_MODEL), D_FF ** -0.5)
    inp['final_norm_w'] = 1.0 + nrm((D_MODEL,), 0.02)
    return inp


def reference(x_prompt, x_sample, state_ssd, state_ssd_conv, state_s5_re, state_s5_im, state_hgrn,
              state_rwkv, state_rwkv_shift, state_ffn_conv,
              norm1_w, w_in, ssd_conv_w, ssd_conv_b, ssd_dt_bias, ssd_a_log, ssd_d, ssd_norm_w,
              s5_a_re, s5_a_im, s5_log_dt, s5_b_re, s5_b_im, s5_c_re, s5_c_im, s5_d, s5_glu_w, s5_glu_b,
              hg_lb_raw, hg_norm_w,
              rw_mu, rw_w0, rw_w_up, rw_a0, rw_a_up, rw_g_up, rw_k_k, rw_k_a, rw_r_k, rw_ln_w, rw_ln_b,
              w_br_ssd, w_br_s5, w_br_hg, w_br_rw, w_merge, b_merge, w_out,
              norm2_w, ffn_up, ffn_conv_w, ffn_conv_b, ffn_down, final_norm_w):
    params = {'norm1_w': norm1_w, 'w_in': w_in, 'ssd_conv_w': ssd_conv_w, 'ssd_conv_b': ssd_conv_b,
              'ssd_dt_bias': ssd_dt_bias, 'ssd_a_log': ssd_a_log, 'ssd_d': ssd_d, 'ssd_norm_w': ssd_norm_w,
              's5_a_re': s5_a_re, 's5_a_im': s5_a_im, 's5_log_dt': s5_log_dt, 's5_b_re': s5_b_re,
              's5_b_im': s5_b_im, 's5_c_re': s5_c_re, 's5_c_im': s5_c_im, 's5_d': s5_d,
              's5_glu_w': s5_glu_w, 's5_glu_b': s5_glu_b, 'hg_norm_w': hg_norm_w,
              'rw_mu': rw_mu, 'rw_w0': rw_w0, 'rw_w_up': rw_w_up, 'rw_a0': rw_a0, 'rw_a_up': rw_a_up,
              'rw_g_up': rw_g_up, 'rw_k_k': rw_k_k, 'rw_k_a': rw_k_a, 'rw_r_k': rw_r_k,
              'rw_ln_w': rw_ln_w, 'rw_ln_b': rw_ln_b,
              'w_br_ssd': w_br_ssd, 'w_br_s5': w_br_s5, 'w_br_hg': w_br_hg, 'w_br_rw': w_br_rw,
              'w_merge': w_merge, 'b_merge': b_merge, 'w_out': w_out, 'norm2_w': norm2_w,
              'ffn_up': ffn_up, 'ffn_conv_w': ffn_conv_w, 'ffn_conv_b': ffn_conv_b, 'ffn_down': ffn_down}
    lb_all = jnp.cumsum(jax.nn.softmax(hg_lb_raw.astype(F32), axis=0), axis=0)
    lb_all = lb_all - lb_all[:1]
    prompt_init = [zero_states(x_prompt.shape[0], x_prompt.dtype) for _ in range(DEPTH)]
    sample_states = (state_ssd, state_ssd_conv, state_s5_re, state_s5_im, state_hgrn,
                     state_rwkv, state_rwkv_shift, state_ffn_conv)
    sample_init = [tuple(s[l] for s in sample_states) for l in range(DEPTH)]
    y_prompt, (p_ssd, p_ssd_conv, p_s5_re, p_s5_im, p_hgrn, p_rwkv, p_rwkv_shift, p_ffn_conv) = trunk(
        x_prompt, prompt_init, params, lb_all, final_norm_w)
    y_sample, (s_ssd, s_ssd_conv, s_s5_re, s_s5_im, s_hgrn, s_rwkv, s_rwkv_shift, s_ffn_conv) = trunk(
        x_sample, sample_init, params, lb_all, final_norm_w)
    return (y_prompt, y_sample,
            p_ssd, p_ssd_conv, p_s5_re, p_s5_im, p_hgrn, p_rwkv, p_rwkv_shift, p_ffn_conv,
            s_ssd, s_ssd_conv, s_s5_re, s_s5_im, s_hgrn, s_rwkv, s_rwkv_shift, s_ffn_conv)
```

```python
import functools
import math

import jax
import jax.numpy as jnp
from jax import lax
from jax.experimental import pallas as pl
from jax.experimental.pallas import tpu as pltpu

F32 = jnp.float32
BF16 = jnp.bfloat16

D_MODEL = 1024
DEPTH = 4
SSD_HEADS = 16
SSD_HEADDIM = 64
SSD_D_STATE = 64
SSD_GROUPS = 4
SSD_CONV = 4
SSD_CONV_CH = 1536
S5_WIDTH = 512
S5_GROUPS = 32
S5_GROUP = 16
S5_STATE = 64
S5_CH = S5_GROUPS * S5_STATE
HG_WIDTH = 512
HG_HEADS = 4
HG_HEADDIM = 128
RW_WIDTH = 512
RW_HEADS = 8
RW_HEADDIM = 64
RW_PROJ = 1792
RW_LN_EPS = 64e-5
D_FF = 2816
FFN_CONV = 3
EPS = 1e-6
IN_SIZES = (1024, 1536, 16, 512, 512, 512, 512, 512, 1792)
DT_PAD = 128

VMEM_LIMIT = 56 * 1024 * 1024


def _bdot(a, b):
    return jnp.dot(a.astype(BF16), b.astype(BF16), preferred_element_type=F32)


def _bdot_nt(a, b):
    return lax.dot_general(a.astype(BF16), b.astype(BF16), (((1,), (1,)), ((), ())),
                           preferred_element_type=F32)


def _bdot_tn(a, b):
    return lax.dot_general(a.astype(BF16), b.astype(BF16), (((0,), (0,)), ((), ())),
                           preferred_element_type=F32)


def _split(x, n):
    parts = []
    r = x
    for _ in range(n):
        p = r.astype(BF16)
        parts.append(p)
        r = r - p.astype(F32)
    return parts


def _mask_dot(m, x, n=3):
    out = None
    for p in _split(x, n):
        t = jnp.dot(m, p, preferred_element_type=F32)
        out = t if out is None else out + t
    return out


def _mask_dot_tn(x, m, n=3):
    out = None
    for p in _split(x, n):
        t = lax.dot_general(p, m, (((0,), (0,)), ((), ())), preferred_element_type=F32)
        out = t if out is None else out + t
    return out


def _dot3(a, b):
    ah, al = _split(a, 2)
    bh, bl = _split(b, 2)
    return (jnp.dot(ah, bh, preferred_element_type=F32) + jnp.dot(ah, bl, preferred_element_type=F32)
            + jnp.dot(al, bh, preferred_element_type=F32))


def _rms(x, w):
    return x * lax.rsqrt(jnp.mean(x * x, axis=-1, keepdims=True) + EPS) * w


def _sigmoid(x):
    return jax.nn.sigmoid(x)


def _silu(x):
    return x * jax.nn.sigmoid(x)


def _stack(ref, w, nb):
    v = ref[...]
    if nb == 1:
        return v
    return jnp.concatenate([v[:, j * w:(j + 1) * w] for j in range(nb)], axis=0)


def _unstack(ref, y, w, nb, lseg):
    if nb == 1:
        ref[...] = y
        return
    for j in range(nb):
        ref[:, j * w:(j + 1) * w] = y[j * lseg:(j + 1) * lseg, :]


def _shift_rows_down(ref, dist, n):
    for off in range(0, n, dist):
        m = min(dist, n - off)
        ref[off:off + m, :] = ref[off + dist:off + dist + m, :]


def _iotas(rows):
    i = lax.broadcasted_iota(jnp.int32, (rows, rows), 0)
    j = lax.broadcasted_iota(jnp.int32, (rows, rows), 1)
    return i, j


def _const_spec(shape):
    nd = len(shape)
    return pl.BlockSpec(shape, lambda *_: (0,) * nd)


def _const_in(shape):
    nd = len(shape)
    return pl.BlockSpec(shape, lambda *_: (0,) * nd, pipeline_mode=pl.Buffered(1))


def _params(sem):
    return pltpu.CompilerParams(dimension_semantics=sem, vmem_limit_bytes=VMEM_LIMIT)


def _lb_kernel(raw_ref, o_ref):
    raw = raw_ref[...]
    m = jnp.max(raw, axis=0, keepdims=True)
    e = jnp.exp(raw - m)
    sm = e / jnp.sum(e, axis=0, keepdims=True)
    acc = jnp.zeros_like(sm[0:1])
    rows = []
    for l in range(DEPTH):
        acc = acc + sm[l:l + 1]
        rows.append(acc)
    first = rows[0]
    o_ref[...] = jnp.concatenate([r - first for r in rows], axis=0)


def _lower_bounds(hg_lb_raw):
    return pl.pallas_call(_lb_kernel, out_shape=jax.ShapeDtypeStruct((DEPTH, HG_WIDTH), F32),
                          name="hg_lower_bounds")(hg_lb_raw)


def _s5_prep_kernel(ldt_ref, are_ref, aim_ref, bre_ref, bim_ref, abre_ref, abim_ref, bbre_ref, bbim_ref):
    dt = jnp.exp(ldt_ref[...])
    a_re = are_ref[...]
    a_im = aim_ref[...]
    mag = jnp.exp(dt * a_re)
    ab_re = mag * jnp.cos(dt * a_im)
    ab_im = mag * jnp.sin(dt * a_im)
    den = a_re * a_re + a_im * a_im
    q_re = ((ab_re - 1.0) * a_re + ab_im * a_im) / den
    q_im = (ab_im * a_re - (ab_re - 1.0) * a_im) / den
    abre_ref[...] = ab_re
    abim_ref[...] = ab_im
    for g in range(S5_GROUPS):
        rs = slice(g * S5_GROUP, (g + 1) * S5_GROUP)
        b_re = bre_ref[rs, :]
        b_im = bim_ref[rs, :]
        bbre_ref[rs, :] = q_re[g:g + 1] * b_re - q_im[g:g + 1] * b_im
        bbim_ref[rs, :] = q_re[g:g + 1] * b_im + q_im[g:g + 1] * b_re


def _s5_prep(log_dt, a_re, a_im, b_re, b_im):
    bt_re = jnp.swapaxes(b_re, 1, 2).reshape(S5_WIDTH, S5_STATE)
    bt_im = jnp.swapaxes(b_im, 1, 2).reshape(S5_WIDTH, S5_STATE)
    gn = jax.ShapeDtypeStruct((S5_GROUPS, S5_STATE), F32)
    wn = jax.ShapeDtypeStruct((S5_WIDTH, S5_STATE), F32)
    return pl.pallas_call(_s5_prep_kernel, out_shape=(gn, gn, wn, wn), name="s5_discretise")(
        log_dt.reshape(S5_GROUPS, 1), a_re, a_im, bt_re, bt_im)


def _block_diag_in(bb):
    eye = jnp.eye(S5_GROUPS, dtype=bb.dtype)
    w = bb.reshape(S5_GROUPS, S5_GROUP, 1, S5_STATE) * eye[:, None, :, None]
    return w.reshape(S5_WIDTH, S5_CH)


def _block_diag_out(c):
    eye = jnp.eye(S5_GROUPS, dtype=c.dtype)
    w = jnp.swapaxes(c, 1, 2)[:, :, None, :] * eye[:, None, :, None]
    return w.reshape(S5_CH, S5_WIDTH)


SEG_Z = (0, 1024)
SEG_XBC = (1024, 1536)
SEG_U = (2560, 512)
SEG_HG = (3072, 2048)
SEG_RW = (5120, 1792)
SEG_DT = (6912, DT_PAD)
IN_PACKED = 7040


def _pack_w_in(w_in):
    off = [0]
    for s in IN_SIZES:
        off.append(off[-1] + s)
    z, xbc, dt, u, q, f, i, g, rw = (w_in[:, off[k]:off[k + 1]] for k in range(9))
    dt = jnp.pad(dt, ((0, 0), (0, DT_PAD - dt.shape[1])))
    return jnp.concatenate([z, xbc, u, q, f, i, g, rw, dt], axis=1).astype(BF16)


def _in_kernel(x_ref, nw_ref, w_ref, cw_ref, cb_ref, dtb_ref, mu_ref, conv0_ref, sh0_ref,
               z_ref, xbc_ref, dt_ref, u_ref, hg_ref, rw_ref, convst_ref, shst_ref,
               full_ref, rwfull_ref, *, bsz, tb):
    step = pl.program_id(0)
    lookback = (SSD_CONV - 1) * bsz

    @pl.when(step == 0)
    def _():
        full_ref[0:lookback, :] = conv0_ref[...]
        rwfull_ref[0:bsz, :] = sh0_ref[...]

    xn = _rms(x_ref[...], nw_ref[...]).astype(BF16)

    def proj(seg):
        return jnp.dot(xn, w_ref[:, seg[0]:seg[0] + seg[1]], preferred_element_type=F32)

    z_ref[...] = proj(SEG_Z)
    u_ref[...] = proj(SEG_U)
    hg_ref[...] = proj(SEG_HG)
    dt_ref[...] = jax.nn.softplus(proj(SEG_DT) + dtb_ref[...])

    full_ref[lookback:lookback + tb, :] = proj(SEG_XBC)
    acc = cb_ref[...] + full_ref[0:tb, :] * cw_ref[0:1, :]
    for j in range(1, SSD_CONV):
        acc = acc + full_ref[j * bsz:j * bsz + tb, :] * cw_ref[j:j + 1, :]
    xbc_ref[...] = _silu(acc)
    _shift_rows_down(full_ref, tb, lookback)
    convst_ref[...] = full_ref[0:lookback, :]

    rwfull_ref[bsz:bsz + tb, :] = proj(SEG_RW)
    cur = rwfull_ref[bsz:bsz + tb, :]
    prev = rwfull_ref[0:tb, :]
    rw_ref[...] = cur + (prev - cur) * mu_ref[...]
    last = rwfull_ref[tb:tb + bsz, :]
    rwfull_ref[0:bsz, :] = last
    shst_ref[...] = last


def _in_proj(x, bsz, tb, norm_w, w_packed, conv_w, conv_b, dt_bias, mu, conv0, shift0):
    rows = x.shape[0]
    lookback = (SSD_CONV - 1) * bsz
    assert rows % tb == 0 and tb % bsz == 0
    row = lambda w: pl.BlockSpec((tb, w), lambda i: (i, 0))
    widths = (1024, SSD_CONV_CH, DT_PAD, S5_WIDTH, 4 * HG_WIDTH, RW_PROJ)
    out_shape = tuple(jax.ShapeDtypeStruct((rows, w), F32) for w in widths) + (
        jax.ShapeDtypeStruct((lookback, SSD_CONV_CH), F32), jax.ShapeDtypeStruct((bsz, RW_PROJ), F32))
    out_specs = tuple(row(w) for w in widths) + (_const_spec((lookback, SSD_CONV_CH)), _const_spec((bsz, RW_PROJ)))
    in_specs = [row(D_MODEL), _const_in((1, D_MODEL)), _const_in((D_MODEL, IN_PACKED)),
                _const_in((SSD_CONV, SSD_CONV_CH)), _const_in((1, SSD_CONV_CH)), _const_in((1, DT_PAD)),
                _const_in((1, RW_PROJ)), _const_in((lookback, SSD_CONV_CH)), _const_in((bsz, RW_PROJ))]
    return pl.pallas_call(
        functools.partial(_in_kernel, bsz=bsz, tb=tb),
        grid=(rows // tb,), in_specs=in_specs, out_specs=out_specs, out_shape=out_shape,
        scratch_shapes=[pltpu.VMEM((lookback + tb, SSD_CONV_CH), F32), pltpu.VMEM((bsz + tb, RW_PROJ), F32)],
        compiler_params=_params(("arbitrary",)), name="in_proj",
    )(x, norm_w, w_packed, conv_w, conv_b, dt_bias, mu, conv0, shift0)


def _ssd_kernel(z_ref, xbc_ref, dt_ref, s0_ref, alog_ref, d_ref, nw_ref, y_ref, s_ref, *, lseg, nb):
    @pl.when(pl.program_id(1) == 0)
    def _():
        s_ref[...] = s0_ref[...]

    rows = lseg * nb
    sh = int(math.log2(lseg))
    z = _stack(z_ref, SSD_HEADS * SSD_HEADDIM, nb)
    xbc = _stack(xbc_ref, SSD_CONV_CH, nb)
    dt = _stack(dt_ref, DT_PAD, nb)
    a = dt * (-jnp.exp(alog_ref[...]))
    d_skip = d_ref[...]

    i, j = _iotas(rows)
    same = (i >> sh) == (j >> sh)
    tril = same & (j <= i)
    cum = _mask_dot(tril.astype(BF16), a)
    cum_t = _mask_dot_tn(a, (same & (i <= j)).astype(BF16))
    tot = _mask_dot(same.astype(BF16), a)
    dec_end = jnp.exp(tot - cum)
    dec_in = jnp.exp(cum)
    dec_tot = jnp.exp(tot)

    n_x = SSD_HEADS * SSD_HEADDIM
    gw = SSD_GROUPS * SSD_D_STATE
    ys = []
    gmat = None
    for h in range(SSD_HEADS):
        g = h // (SSD_HEADS // SSD_GROUPS)
        bm = xbc[:, n_x + g * SSD_D_STATE:n_x + (g + 1) * SSD_D_STATE]
        cm = xbc[:, n_x + gw + g * SSD_D_STATE:n_x + gw + (g + 1) * SSD_D_STATE]
        if h % (SSD_HEADS // SSD_GROUPS) == 0:
            gmat = _bdot_nt(cm, bm)
        xh = xbc[:, h * SSD_HEADDIM:(h + 1) * SSD_HEADDIM]
        xdt = xh * dt[:, h:h + 1]
        diff = cum[:, h:h + 1] - cum_t[h:h + 1, :]
        lmat = jnp.where(tril, jnp.exp(jnp.minimum(diff, 0.0)), 0.0)
        y = _bdot(gmat * lmat, xdt)
        xdec = xdt * dec_end[:, h:h + 1]
        offs = []
        for s in range(nb):
            rs = slice(s * lseg, (s + 1) * lseg)
            st = s_ref[s, h]
            offs.append(_bdot_nt(cm[rs], st) * dec_in[rs, h:h + 1])
            s_ref[s, h] = st * dec_tot[s * lseg:s * lseg + 1, h:h + 1] + _bdot_tn(xdec[rs], bm[rs])
        y_off = offs[0] if nb == 1 else jnp.concatenate(offs, axis=0)
        ys.append(y + y_off + xh * d_skip[:, h:h + 1])
    yall = jnp.concatenate(ys, axis=1)
    yall = _rms(yall * _silu(z), nw_ref[...])
    _unstack(y_ref, yall, n_x, nb, lseg)


def _ssd(z, xbc, dt, s0, a_log, d_skip, norm_w, bsz, seqlen, lseg, nb):
    n_x = SSD_HEADS * SSD_HEADDIM
    view = lambda t, w: t.reshape(seqlen, bsz * w)
    blk = lambda w: pl.BlockSpec((lseg, nb * w), lambda b, c: (c, b))
    st = pl.BlockSpec((nb, SSD_HEADS, SSD_HEADDIM, SSD_D_STATE), lambda b, c: (b, 0, 0, 0))
    y, s = pl.pallas_call(
        functools.partial(_ssd_kernel, lseg=lseg, nb=nb),
        grid=(bsz // nb, seqlen // lseg),
        in_specs=[blk(n_x), blk(SSD_CONV_CH), blk(DT_PAD), st,
                  _const_in((1, DT_PAD)), _const_in((1, DT_PAD)), _const_in((1, n_x))],
        out_specs=(blk(n_x), st),
        out_shape=(jax.ShapeDtypeStruct((seqlen, bsz * n_x), F32), jax.ShapeDtypeStruct(s0.shape, F32)),
        compiler_params=_params(("arbitrary", "arbitrary")), name="ssd_mixer",
    )(view(z, n_x), view(xbc, SSD_CONV_CH), view(dt, DT_PAD), s0, a_log, d_skip, norm_w)
    return y.reshape(seqlen * bsz, n_x), s


def _s5_kernel(u_ref, win_ref, wcre_ref, wcim_ref, abre_ref, abim_ref, d_ref, gw_ref, gb_ref, hr0_ref, hi0_ref,
               y_ref, hr_ref, hi_ref, bu_ref, *, bsz, tt):
    @pl.when(pl.program_id(0) == 0)
    def _():
        hr_ref[...] = hr0_ref[...]
        hi_ref[...] = hi0_ref[...]

    u = u_ref[...]
    bu_ref[...] = jnp.dot(u.astype(BF16), win_ref[...], preferred_element_type=F32)
    ab_re = abre_ref[...]
    ab_im = abim_ref[...]

    def step(t, carry):
        r0 = pl.multiple_of(t * bsz, bsz)
        hr = hr_ref[...]
        hi = hi_ref[...]
        nr = ab_re * hr - ab_im * hi + bu_ref[pl.ds(r0, bsz), 0:S5_CH]
        ni = ab_re * hi + ab_im * hr + bu_ref[pl.ds(r0, bsz), S5_CH:2 * S5_CH]
        hr_ref[...] = nr
        hi_ref[...] = ni
        bu_ref[pl.ds(r0, bsz), 0:S5_CH] = nr
        bu_ref[pl.ds(r0, bsz), S5_CH:2 * S5_CH] = ni
        return carry

    lax.fori_loop(0, tt, step, 0)
    y = (jnp.dot(bu_ref[:, 0:S5_CH].astype(BF16), wcre_ref[...], preferred_element_type=F32)
         - jnp.dot(bu_ref[:, S5_CH:2 * S5_CH].astype(BF16), wcim_ref[...], preferred_element_type=F32))
    y = jax.nn.gelu(y + d_ref[...] * u)
    y_ref[...] = y * _sigmoid(jnp.dot(y.astype(BF16), gw_ref[...], preferred_element_type=F32) + gb_ref[...])


def _s5(u, bsz, tt, w_in_bd, w_cre_bd, w_cim_bd, ab_re, ab_im, d_skip, glu_w, glu_b, hr0, hi0):
    rows = u.shape[0]
    tb = tt * bsz
    assert rows % tb == 0
    row = pl.BlockSpec((tb, S5_WIDTH), lambda i: (i, 0))
    st = _const_spec((bsz, S5_CH))
    st_in = _const_in((bsz, S5_CH))
    return pl.pallas_call(
        functools.partial(_s5_kernel, bsz=bsz, tt=tt),
        grid=(rows // tb,),
        in_specs=[row, _const_in((S5_WIDTH, 2 * S5_CH)), _const_in((S5_CH, S5_WIDTH)),
                  _const_in((S5_CH, S5_WIDTH)), _const_in((1, S5_CH)), _const_in((1, S5_CH)),
                  _const_in((1, S5_WIDTH)), _const_in((S5_WIDTH, S5_WIDTH)), _const_in((1, S5_WIDTH)), st_in, st_in],
        out_specs=(row, st, st),
        out_shape=(jax.ShapeDtypeStruct((rows, S5_WIDTH), F32), jax.ShapeDtypeStruct((bsz, S5_CH), F32),
                   jax.ShapeDtypeStruct((bsz, S5_CH), F32)),
        scratch_shapes=[pltpu.VMEM((tb, 2 * S5_CH), F32)],
        compiler_params=_params(("arbitrary",)), name="s5_mixer",
    )(u, w_in_bd, w_cre_bd, w_cim_bd, ab_re, ab_im, d_skip, glu_w, glu_b, hr0, hi0)


def _hg_kernel(x_ref, s0_ref, lb_ref, nw_ref, y_ref, s_ref, *, lseg, nb):
    @pl.when(pl.program_id(1) == 0)
    def _():
        s_ref[...] = s0_ref[...]

    rows = lseg * nb
    nlev = int(math.log2(lseg))
    x = _stack(x_ref, 4 * HG_WIDTH, nb)
    q = _silu(x[:, 0:HG_WIDTH])
    lb = lb_ref[...]
    f = lb + (1.0 - lb) * _sigmoid(x[:, HG_WIDTH:2 * HG_WIDTH])
    logf = jnp.log(f)
    k = 1.0 - f
    v = x[:, 2 * HG_WIDTH:3 * HG_WIDTH]
    gate = _sigmoid(x[:, 3 * HG_WIDTH:4 * HG_WIDTH])

    i, j = _iotas(rows)
    hd = lambda t, h: t[:, h * HG_HEADDIM:(h + 1) * HG_HEADDIM]

    att = [None] * HG_HEADS
    for l in range(nlev):
        bi = i >> l
        bj = j >> l
        if l == 0:
            qe, ke = q * f, k
        else:
            blk = bi == bj
            qe = q * jnp.exp(_mask_dot((blk & (j <= i)).astype(BF16), logf))
            ke = k * jnp.exp(_mask_dot((blk & (j > i)).astype(BF16), logf))
        pair = (bi == bj + 1) & ((bi & 1) == 1)
        for h in range(HG_HEADS):
            t = jnp.where(pair, _bdot_nt(hd(qe, h), hd(ke, h)), 0.0)
            att[h] = t if att[h] is None else att[h] + t

    same = (i >> nlev) == (j >> nlev)
    qin = q * jnp.exp(_mask_dot((same & (j <= i)).astype(BF16), logf))
    kend = k * jnp.exp(_mask_dot((same & (j > i)).astype(BF16), logf))
    ones = jnp.ones((lseg, HG_HEADDIM), BF16)
    outs = []
    for h in range(HG_HEADS):
        qh, kh, vh = hd(q, h), hd(k, h), hd(v, h)
        diag = jnp.sum(qh * kh, axis=-1, keepdims=True)
        a_h = jnp.where(i == j, diag, 0.0)
        if att[h] is not None:
            a_h = a_h + att[h]
        o = _bdot(a_h, vh)
        inter = []
        for s in range(nb):
            rs = slice(s * lseg, (s + 1) * lseg)
            st = s_ref[s, h]
            inter.append(_bdot(hd(qin, h)[rs], st))
            tot = _mask_dot_tn(hd(logf, h)[rs], ones)
            s_ref[s, h] = jnp.exp(tot) * st + _bdot_tn(hd(kend, h)[rs], vh[rs])
        o = o + (inter[0] if nb == 1 else jnp.concatenate(inter, axis=0))
        outs.append(_rms(o, nw_ref[...]) * hd(gate, h))
    _unstack(y_ref, jnp.concatenate(outs, axis=1), HG_WIDTH, nb, lseg)


def _hgrn(x, s0, lb, norm_w, bsz, seqlen, lseg, nb):
    w_in = 4 * HG_WIDTH
    st = pl.BlockSpec((nb, HG_HEADS, HG_HEADDIM, HG_HEADDIM), lambda b, c: (b, 0, 0, 0))
    y, s = pl.pallas_call(
        functools.partial(_hg_kernel, lseg=lseg, nb=nb),
        grid=(bsz // nb, seqlen // lseg),
        in_specs=[pl.BlockSpec((lseg, nb * w_in), lambda b, c: (c, b)), st,
                  _const_in((1, HG_WIDTH)), _const_in((1, HG_HEADDIM))],
        out_specs=(pl.BlockSpec((lseg, nb * HG_WIDTH), lambda b, c: (c, b)), st),
        out_shape=(jax.ShapeDtypeStruct((seqlen, bsz * HG_WIDTH), F32), jax.ShapeDtypeStruct(s0.shape, F32)),
        compiler_params=_params(("arbitrary", "arbitrary")), name="hgrn_mixer",
    )(x.reshape(seqlen, bsz * w_in), s0, lb, norm_w)
    return y.reshape(seqlen * bsz, HG_WIDTH), s


def _rw_kernel(x_ref, s0_ref, w0_ref, wup_ref, a0_ref, aup_ref, gup_ref, kk_ref, ka_ref, rk_ref, lnw_ref, lnb_ref,
               y_ref, s_ref, *, lseg, nb):
    @pl.when(pl.program_id(1) == 0)
    def _():
        s_ref[...] = s0_ref[...]

    rows = lseg * nb
    nlev = int(math.log2(lseg))
    x = _stack(x_ref, RW_PROJ, nb)
    W = RW_WIDTH
    r, k, v = x[:, 0:W], x[:, W:2 * W], x[:, 2 * W:3 * W]
    wd, ad, gd = x[:, 3 * W:3 * W + 64], x[:, 3 * W + 64:3 * W + 128], x[:, 3 * W + 128:3 * W + 256]
    w = -jax.nn.softplus(-(w0_ref[...] + _bdot(jnp.tanh(wd), wup_ref[...]))) - 0.5
    logw = -jnp.exp(w)
    ag = _sigmoid(a0_ref[...] + _bdot(ad, aup_ref[...]))
    g = _bdot(_sigmoid(gd), gup_ref[...])

    i, j = _iotas(rows)
    same = (i >> nlev) == (j >> nlev)
    incl = same & (j <= i)
    strict = same & (j < i)
    b_in = _mask_dot(incl.astype(BF16), logw)
    b_tot = _mask_dot(same.astype(BF16), logw)
    e_in = jnp.exp(b_in)
    e_ex = jnp.exp(b_in - logw)
    e_neg = jnp.exp(-b_in)
    e_end = jnp.exp(b_tot - b_in)
    e_tot = jnp.exp(b_tot)
    eye = (i == j).astype(F32)

    hd = lambda t, h: t[:, h * RW_HEADDIM:(h + 1) * RW_HEADDIM]
    outs = []
    for h in range(RW_HEADS):
        rh, kh, vh, agh = hd(r, h), hd(k, h), hd(v, h), hd(ag, h)
        kk = kh * hd(kk_ref[...], h)
        kk = kk * lax.rsqrt(jnp.maximum(jnp.sum(kk * kk, axis=-1, keepdims=True), 1e-24))
        k2 = kh * (1.0 + (agh - 1.0) * hd(ka_ref[...], h))
        a_t = (-kk) * hd(e_ex, h)
        b_n = (kk * agh) * hd(e_neg, h)
        k_n = k2 * hd(e_neg, h)
        r_t = rh * hd(e_in, h)
        a_ab = jnp.where(strict, _bdot_nt(a_t, b_n), 0.0)
        a_ak = jnp.where(strict, _bdot_nt(a_t, k_n), 0.0)
        a_rb = jnp.where(incl, _bdot_nt(r_t, b_n), 0.0)
        a_rk = jnp.where(incl, _bdot_nt(r_t, k_n), 0.0)
        states = [s_ref[s, h] for s in range(nb)]
        cat = lambda parts: parts[0] if nb == 1 else jnp.concatenate(parts, axis=0)
        seg = lambda t, s: t[s * lseg:(s + 1) * lseg]
        zmat = cat([_bdot_nt(seg(a_t, s), states[s]) for s in range(nb)]) + _bdot(a_ak, vh)
        inv = eye
        for l in range(nlev):
            bi = i >> l
            bj = j >> l
            off = jnp.where((bi == bj + 1) & ((bi & 1) == 1), a_ab, 0.0)
            inv = inv + off if l == 0 else inv + _dot3(inv, _dot3(off, inv))
        sa = _dot3(inv, zmat)
        y = cat([_bdot_nt(seg(r_t, s), states[s]) for s in range(nb)]) + _bdot(a_rb, sa) + _bdot(a_rk, vh)
        b_e = (kk * agh) * hd(e_end, h)
        k_e = k2 * hd(e_end, h)
        for s in range(nb):
            s_ref[s, h] = (states[s] * hd(e_tot, h)[s * lseg:s * lseg + 1]
                           + _bdot_tn(seg(sa, s), seg(b_e, s)) + _bdot_tn(seg(vh, s), seg(k_e, s)))
        mu = jnp.mean(y, axis=-1, keepdims=True)
        yc = y - mu
        var = jnp.mean(yc * yc, axis=-1, keepdims=True)
        y = yc * lax.rsqrt(var + RW_LN_EPS) * hd(lnw_ref[...], h) + hd(lnb_ref[...], h)
        y = y + jnp.sum(rh * k2 * hd(rk_ref[...], h), axis=-1, keepdims=True) * vh
        outs.append(y)
    _unstack(y_ref, jnp.concatenate(outs, axis=1) * g, RW_WIDTH, nb, lseg)


def _rwkv(x, s0, p, bsz, seqlen, lseg, nb):
    st = pl.BlockSpec((nb, RW_HEADS, RW_HEADDIM, RW_HEADDIM), lambda b, c: (b, 0, 0, 0))
    vec = _const_in((1, RW_WIDTH))
    y, s = pl.pallas_call(
        functools.partial(_rw_kernel, lseg=lseg, nb=nb),
        grid=(bsz // nb, seqlen // lseg),
        in_specs=[pl.BlockSpec((lseg, nb * RW_PROJ), lambda b, c: (c, b)), st,
                  vec, _const_in((64, RW_WIDTH)), vec, _const_in((64, RW_WIDTH)), _const_in((128, RW_WIDTH)),
                  vec, vec, vec, vec, vec],
        out_specs=(pl.BlockSpec((lseg, nb * RW_WIDTH), lambda b, c: (c, b)), st),
        out_shape=(jax.ShapeDtypeStruct((seqlen, bsz * RW_WIDTH), F32), jax.ShapeDtypeStruct(s0.shape, F32)),
        compiler_params=_params(("arbitrary", "arbitrary")), name="rwkv_mixer",
    )(x.reshape(seqlen, bsz * RW_PROJ), s0, p['w0'], p['w_up'], p['a0'], p['a_up'], p['g_up'],
      p['k_k'], p['k_a'], p['r_k'], p['ln_w'], p['ln_b'])
    return y.reshape(seqlen * bsz, RW_WIDTH), s


def _merge_kernel(x_ref, yssd_ref, ys5_ref, yhg_ref, yrw_ref, nw_ref, wm_ref, bm_ref,
                  wssd_ref, ws5_ref, whg_ref, wrw_ref, wout_ref, o_ref):
    x = x_ref[...]
    xn = _rms(x, nw_ref[...]).astype(BF16)
    merged = None
    for b, (y_ref, w_ref) in enumerate(((yssd_ref, wssd_ref), (ys5_ref, ws5_ref), (yhg_ref, whg_ref), (yrw_ref, wrw_ref))):
        cs = slice(b * D_MODEL, (b + 1) * D_MODEL)
        gate = _sigmoid(jnp.dot(xn, wm_ref[:, cs], preferred_element_type=F32) + bm_ref[:, cs])
        t = gate * jnp.dot(y_ref[...].astype(BF16), w_ref[...], preferred_element_type=F32)
        merged = t if merged is None else merged + t
    o_ref[...] = x + jnp.dot(merged.astype(BF16), wout_ref[...], preferred_element_type=F32)


def _merge(x, y_ssd, y_s5, y_hg, y_rw, tb, norm_w, w_merge, b_merge, w_ssd, w_s5, w_hg, w_rw, w_out):
    rows = x.shape[0]
    row = lambda w: pl.BlockSpec((tb, w), lambda i: (i, 0))
    return pl.pallas_call(
        _merge_kernel, grid=(rows // tb,),
        in_specs=[row(D_MODEL), row(1024), row(S5_WIDTH), row(HG_WIDTH), row(RW_WIDTH),
                  _const_in((1, D_MODEL)), _const_in((D_MODEL, 4 * D_MODEL)), _const_in((1, 4 * D_MODEL)),
                  _const_in((1024, D_MODEL)), _const_in((S5_WIDTH, D_MODEL)), _const_in((HG_WIDTH, D_MODEL)),
                  _const_in((RW_WIDTH, D_MODEL)), _const_in((D_MODEL, D_MODEL))],
        out_specs=row(D_MODEL), out_shape=jax.ShapeDtypeStruct((rows, D_MODEL), F32),
        compiler_params=_params(("arbitrary",)), name="merge",
    )(x, y_ssd, y_s5, y_hg, y_rw, norm_w, w_merge, b_merge, w_ssd, w_s5, w_hg, w_rw, w_out)


def _ffn_kernel(x_ref, nw_ref, wup_ref, cw_ref, cb_ref, wdn_ref, conv0_ref, fnw_ref, o_ref, convst_ref, full_ref,
                *, bsz, tb, final_norm):
    lookback = (FFN_CONV - 1) * bsz

    @pl.when(pl.program_id(0) == 0)
    def _():
        full_ref[0:lookback, :] = conv0_ref[...]

    x = x_ref[...]
    xn = _rms(x, nw_ref[...]).astype(BF16)
    full_ref[lookback:lookback + tb, :] = jnp.dot(xn, wup_ref[...], preferred_element_type=F32)
    acc = cb_ref[...] + full_ref[0:tb, :] * cw_ref[0:1, :]
    for j in range(1, FFN_CONV):
        acc = acc + full_ref[j * bsz:j * bsz + tb, :] * cw_ref[j:j + 1, :]
    _shift_rows_down(full_ref, tb, lookback)
    convst_ref[...] = full_ref[0:lookback, :]
    hidden = jax.nn.gelu(acc[:, 0:D_FF]) * acc[:, D_FF:2 * D_FF]
    out = x + jnp.dot(hidden.astype(BF16), wdn_ref[...], preferred_element_type=F32)
    if final_norm:
        out = _rms(out, fnw_ref[...])
    o_ref[...] = out


def _ffn(x, bsz, tb, norm_w, w_up, conv_w, conv_b, w_down, conv0, final_w, final_norm):
    rows = x.shape[0]
    lookback = (FFN_CONV - 1) * bsz
    assert rows % tb == 0 and tb % bsz == 0
    row = pl.BlockSpec((tb, D_MODEL), lambda i: (i, 0))
    return pl.pallas_call(
        functools.partial(_ffn_kernel, bsz=bsz, tb=tb, final_norm=final_norm),
        grid=(rows // tb,),
        in_specs=[row, _const_in((1, D_MODEL)), _const_in((D_MODEL, 2 * D_FF)), _const_in((FFN_CONV, 2 * D_FF)),
                  _const_in((1, 2 * D_FF)), _const_in((D_FF, D_MODEL)), _const_in((lookback, 2 * D_FF)),
                  _const_in((1, D_MODEL))],
        out_specs=(row, _const_spec((lookback, 2 * D_FF))),
        out_shape=(jax.ShapeDtypeStruct((rows, D_MODEL), F32), jax.ShapeDtypeStruct((lookback, 2 * D_FF), F32)),
        scratch_shapes=[pltpu.VMEM((lookback + tb, 2 * D_FF), F32)],
        compiler_params=_params(("arbitrary",)), name="conv_ffn",
    )(x, norm_w, w_up, conv_w, conv_b, w_down, conv0, final_w)


def _pad_lanes(v, width=DT_PAD):
    return jnp.pad(v, (0, width - v.shape[0])).reshape(1, width)


def _layer_params(l, P, lb_all):
    p = {n: a[l] for n, a in P.items()}
    row = lambda a: a.reshape(1, -1)
    ab_re, ab_im, bb_re, bb_im = _s5_prep(p['s5_log_dt'], p['s5_a_re'], p['s5_a_im'], p['s5_b_re'], p['s5_b_im'])
    q = dict(
        norm1_w=row(p['norm1_w']), w_in=_pack_w_in(p['w_in']),
        ssd_conv_w=p['ssd_conv_w'], ssd_conv_b=row(p['ssd_conv_b']), ssd_dt_bias=_pad_lanes(p['ssd_dt_bias']),
        ssd_a_log=_pad_lanes(p['ssd_a_log']), ssd_d=_pad_lanes(p['ssd_d']), ssd_norm_w=row(p['ssd_norm_w']),
        s5_win=jnp.concatenate([_block_diag_in(bb_re), _block_diag_in(bb_im)], axis=1).astype(BF16),
        s5_wcre=_block_diag_out(p['s5_c_re']).astype(BF16), s5_wcim=_block_diag_out(p['s5_c_im']).astype(BF16),
        s5_ab_re=row(ab_re), s5_ab_im=row(ab_im), s5_d=row(p['s5_d']),
        s5_glu_w=p['s5_glu_w'].astype(BF16), s5_glu_b=row(p['s5_glu_b']),
        hg_lb=lb_all[l:l + 1], hg_norm_w=row(p['hg_norm_w']),
        rw_mu=row(p['rw_mu']),
        rw=dict(w0=row(p['rw_w0']), w_up=p['rw_w_up'].astype(BF16), a0=row(p['rw_a0']), a_up=p['rw_a_up'].astype(BF16),
                g_up=p['rw_g_up'].astype(BF16), k_k=row(p['rw_k_k']), k_a=row(p['rw_k_a']), r_k=row(p['rw_r_k']),
                ln_w=row(p['rw_ln_w']), ln_b=row(p['rw_ln_b'])),
        w_merge=p['w_merge'].astype(BF16), b_merge=row(p['b_merge']),
        w_br_ssd=p['w_br_ssd'].astype(BF16), w_br_s5=p['w_br_s5'].astype(BF16), w_br_hg=p['w_br_hg'].astype(BF16),
        w_br_rw=p['w_br_rw'].astype(BF16), w_out=p['w_out'].astype(BF16),
        norm2_w=row(p['norm2_w']), ffn_up=p['ffn_up'].astype(BF16), ffn_conv_w=p['ffn_conv_w'],
        ffn_conv_b=row(p['ffn_conv_b']), ffn_down=p['ffn_down'].astype(BF16),
    )
    return q


class _Group:
    def __init__(self, bsz, seqlen):
        self.bsz, self.seqlen = bsz, seqlen
        rows = bsz * seqlen
        self.tb = min(rows, max(256, bsz))
        self.tt = self.tb // bsz
        if seqlen >= 128:
            self.nb, self.ssd_l, self.hg_l, self.rw_l = 1, 128, 64, 64
        else:
            self.nb, self.ssd_l, self.hg_l, self.rw_l = 8, seqlen, seqlen, seqlen


def _time_major(a):
    a = jnp.swapaxes(a, 0, 1)
    return a.reshape((a.shape[0] * a.shape[1],) + a.shape[2:])


def _batch_major(a, bsz):
    a = a.reshape((a.shape[0] // bsz, bsz) + a.shape[1:])
    return jnp.swapaxes(a, 0, 1)


def _trunk(x, states, layers, final_norm_w, grp):
    bsz, seqlen = grp.bsz, grp.seqlen
    xt = _time_major(x)
    new_states = []
    for l, q in enumerate(layers):
        s_ssd, s_conv, s_s5r, s_s5i, s_hg, s_rw, s_shift, s_fconv = states[l]
        z, xbc, dt, u, hg, rwx, conv_new, shift_new = _in_proj(
            xt, bsz, grp.tb, q['norm1_w'], q['w_in'], q['ssd_conv_w'], q['ssd_conv_b'], q['ssd_dt_bias'], q['rw_mu'],
            _time_major(s_conv), s_shift)
        y_ssd, ssd_new = _ssd(z, xbc, dt, s_ssd, q['ssd_a_log'], q['ssd_d'], q['ssd_norm_w'],
                              bsz, seqlen, grp.ssd_l, grp.nb)
        y_s5, hr_new, hi_new = _s5(u, bsz, grp.tt, q['s5_win'], q['s5_wcre'], q['s5_wcim'], q['s5_ab_re'], q['s5_ab_im'],
                                   q['s5_d'], q['s5_glu_w'], q['s5_glu_b'],
                                   s_s5r.reshape(bsz, S5_CH), s_s5i.reshape(bsz, S5_CH))
        y_hg, hg_new = _hgrn(hg, s_hg, q['hg_lb'], q['hg_norm_w'], bsz, seqlen, grp.hg_l, grp.nb)
        y_rw, rw_new = _rwkv(rwx, s_rw, q['rw'], bsz, seqlen, grp.rw_l, grp.nb)
        x1 = _merge(xt, y_ssd, y_s5, y_hg, y_rw, grp.tb, q['norm1_w'], q['w_merge'], q['b_merge'],
                    q['w_br_ssd'], q['w_br_s5'], q['w_br_hg'], q['w_br_rw'], q['w_out'])
        xt, fconv_new = _ffn(x1, bsz, grp.tb, q['norm2_w'], q['ffn_up'], q['ffn_conv_w'], q['ffn_conv_b'],
                             q['ffn_down'], _time_major(s_fconv), final_norm_w.reshape(1, D_MODEL), l == DEPTH - 1)
        new_states.append((ssd_new, _batch_major(conv_new, bsz),
                           hr_new.reshape(bsz, S5_GROUPS, S5_STATE), hi_new.reshape(bsz, S5_GROUPS, S5_STATE),
                           hg_new, rw_new, shift_new, _batch_major(fconv_new, bsz)))
    stacked = tuple(jnp.stack([st[k] for st in new_states], axis=0) for k in range(8))
    return _batch_major(xt, bsz), stacked


def _zero_states(bsz):
    z = lambda *s: jnp.zeros((bsz,) + s, F32)
    return (z(SSD_HEADS, SSD_HEADDIM, SSD_D_STATE), z(SSD_CONV - 1, SSD_CONV_CH), z(S5_GROUPS, S5_STATE),
            z(S5_GROUPS, S5_STATE), z(HG_HEADS, HG_HEADDIM, HG_HEADDIM), z(RW_HEADS, RW_HEADDIM, RW_HEADDIM),
            z(RW_PROJ), z(FFN_CONV - 1, 2 * D_FF))


def kernel(x_prompt, x_sample, state_ssd, state_ssd_conv, state_s5_re, state_s5_im, state_hgrn, state_rwkv, state_rwkv_shift, state_ffn_conv, norm1_w, w_in, ssd_conv_w, ssd_conv_b, ssd_dt_bias, ssd_a_log, ssd_d, ssd_norm_w, s5_a_re, s5_a_im, s5_log_dt, s5_b_re, s5_b_im, s5_c_re, s5_c_im, s5_d, s5_glu_w, s5_glu_b, hg_lb_raw, hg_norm_w, rw_mu, rw_w0, rw_w_up, rw_a0, rw_a_up, rw_g_up, rw_k_k, rw_k_a, rw_r_k, rw_ln_w, rw_ln_b, w_br_ssd, w_br_s5, w_br_hg, w_br_rw, w_merge, b_merge, w_out, norm2_w, ffn_up, ffn_conv_w, ffn_conv_b, ffn_down, final_norm_w):
    P = dict(norm1_w=norm1_w, w_in=w_in, ssd_conv_w=ssd_conv_w, ssd_conv_b=ssd_conv_b, ssd_dt_bias=ssd_dt_bias,
             ssd_a_log=ssd_a_log, ssd_d=ssd_d, ssd_norm_w=ssd_norm_w, s5_a_re=s5_a_re, s5_a_im=s5_a_im,
             s5_log_dt=s5_log_dt, s5_b_re=s5_b_re, s5_b_im=s5_b_im, s5_c_re=s5_c_re, s5_c_im=s5_c_im, s5_d=s5_d,
             s5_glu_w=s5_glu_w, s5_glu_b=s5_glu_b, hg_norm_w=hg_norm_w, rw_mu=rw_mu, rw_w0=rw_w0, rw_w_up=rw_w_up,
             rw_a0=rw_a0, rw_a_up=rw_a_up, rw_g_up=rw_g_up,
             rw_k_k=rw_k_k.reshape(DEPTH, RW_WIDTH), rw_k_a=rw_k_a.reshape(DEPTH, RW_WIDTH),
             rw_r_k=rw_r_k.reshape(DEPTH, RW_WIDTH), rw_ln_w=rw_ln_w.reshape(DEPTH, RW_WIDTH),
             rw_ln_b=rw_ln_b.reshape(DEPTH, RW_WIDTH),
             w_br_ssd=w_br_ssd, w_br_s5=w_br_s5, w_br_hg=w_br_hg, w_br_rw=w_br_rw, w_merge=w_merge, b_merge=b_merge,
             w_out=w_out, norm2_w=norm2_w, ffn_up=ffn_up, ffn_conv_w=ffn_conv_w, ffn_conv_b=ffn_conv_b,
             ffn_down=ffn_down)
    lb_all = _lower_bounds(hg_lb_raw)
    layers = [_layer_params(l, P, lb_all) for l in range(DEPTH)]
    sample_states = (state_ssd, state_ssd_conv, state_s5_re, state_s5_im, state_hgrn, state_rwkv,
                     state_rwkv_shift, state_ffn_conv)
    sample_init = [tuple(s[l] for s in sample_states) for l in range(DEPTH)]
    prompt_init = [_zero_states(x_prompt.shape[0])] * DEPTH
    y_prompt, p_states = _trunk(x_prompt, prompt_init, layers, final_norm_w, _Group(*x_prompt.shape[:2]))
    y_sample, s_states = _trunk(x_sample, sample_init, layers, final_norm_w, _Group(*x_sample.shape[:2]))
    return (y_prompt, y_sample) + p_states + s_states
```

```python
import functools
import math

import jax
import jax.numpy as jnp
from jax import lax
from jax.experimental import pallas as pl
from jax.experimental.pallas import tpu as pltpu

F32 = jnp.float32
BF16 = jnp.bfloat16

D_MODEL = 1024
DEPTH = 4
SSD_HEADS = 16
SSD_HEADDIM = 64
SSD_D_STATE = 64
SSD_GROUPS = 4
SSD_CONV = 4
SSD_CONV_CH = 1536
S5_WIDTH = 512
S5_GROUPS = 32
S5_GROUP = 16
S5_STATE = 64
S5_CH = S5_GROUPS * S5_STATE
HG_WIDTH = 512
HG_HEADS = 4
HG_HEADDIM = 128
RW_WIDTH = 512
RW_HEADS = 8
RW_HEADDIM = 64
RW_PROJ = 1792
RW_LN_EPS = 64e-5
D_FF = 2816
FFN_CONV = 3
EPS = 1e-6
IN_SIZES = (1024, 1536, 16, 512, 512, 512, 512, 512, 1792)
DT_PAD = 128
SUB = 8

VMEM_LIMIT = 56 * 1024 * 1024


def _bdot(a, b):
    return jnp.dot(a.astype(BF16), b.astype(BF16), preferred_element_type=F32)


def _split(x, n):
    parts = []
    r = x
    for _ in range(n):
        p = r.astype(BF16)
        parts.append(p)
        r = r - p.astype(F32)
    return parts


def _mask_dot(m, x, n=3):
    out = None
    for p in _split(x, n):
        t = jnp.dot(m, p, preferred_element_type=F32)
        out = t if out is None else out + t
    return out


def _dot_mask(x, m, n=3):
    out = None
    for p in _split(x, n):
        t = jnp.dot(p, m, preferred_element_type=F32)
        out = t if out is None else out + t
    return out


def _rms(x, w):
    return x * lax.rsqrt(jnp.mean(x * x, axis=-1, keepdims=True) + EPS) * w


def _sigmoid(x):
    return jax.nn.sigmoid(x)


def _silu(x):
    return x * jax.nn.sigmoid(x)


def _load_rows(ref, p, nb, dense_ref):
    lseg = ref.shape[0]
    for s in range(nb):
        dense_ref[s * lseg:(s + 1) * lseg, :] = ref[:, p * nb + s, :]
    return dense_ref[...]


def _store_rows(ref, p, y, nb, lseg):
    for s in range(nb):
        ref[:, p * nb + s, :] = y[s * lseg:(s + 1) * lseg, :]


def _init_states_transposed(s_ref, s0_ref, nheads):
    @pl.when(pl.program_id(1) == 0)
    def _():
        def body(b, carry):
            for h in range(nheads):
                s_ref[b, h] = s0_ref[b, h].T
            return carry
        lax.fori_loop(0, SUB, body, 0)


def _finish_states_transposed(s_ref, nheads):
    @pl.when(pl.program_id(1) == pl.num_programs(1) - 1)
    def _():
        def body(b, carry):
            for h in range(nheads):
                s_ref[b, h] = s_ref[b, h].T
            return carry
        lax.fori_loop(0, SUB, body, 0)


def _for_each_problem(nb, problem):
    if nb == SUB:
        problem(0)
    else:
        def body(p, carry):
            problem(p)
            return carry
        lax.fori_loop(0, SUB // nb, body, 0)


def _shift_rows_down(ref, dist, n):
    for off in range(0, n, dist):
        m = min(dist, n - off)
        ref[off:off + m, :] = ref[off + dist:off + dist + m, :]


def _iotas(rows):
    i = lax.broadcasted_iota(jnp.int32, (rows, rows), 0)
    j = lax.broadcasted_iota(jnp.int32, (rows, rows), 1)
    return i, j


def _const_spec(shape):
    nd = len(shape)
    return pl.BlockSpec(shape, lambda *_: (0,) * nd)


def _const_in(shape):
    nd = len(shape)
    return pl.BlockSpec(shape, lambda *_: (0,) * nd, pipeline_mode=pl.Buffered(1))


def _params(sem):
    return pltpu.CompilerParams(dimension_semantics=sem, vmem_limit_bytes=VMEM_LIMIT)


def _lb_kernel(raw_ref, o_ref):
    raw = raw_ref[...]
    m = jnp.max(raw, axis=0, keepdims=True)
    e = jnp.exp(raw - m)
    sm = e / jnp.sum(e, axis=0, keepdims=True)
    acc = jnp.zeros_like(sm[0:1])
    rows = []
    for l in range(DEPTH):
        acc = acc + sm[l:l + 1]
        rows.append(acc)
    first = rows[0]
    o_ref[...] = jnp.concatenate([r - first for r in rows], axis=0)


def _lower_bounds(hg_lb_raw):
    return pl.pallas_call(_lb_kernel, out_shape=jax.ShapeDtypeStruct((DEPTH, HG_WIDTH), F32),
                          name="hg_lower_bounds")(hg_lb_raw)


def _s5_prep_kernel(ldt_ref, are_ref, aim_ref, bre_ref, bim_ref, abre_ref, abim_ref, bbre_ref, bbim_ref):
    dt = jnp.exp(ldt_ref[...])
    a_re = are_ref[...]
    a_im = aim_ref[...]
    mag = jnp.exp(dt * a_re)
    ab_re = mag * jnp.cos(dt * a_im)
    ab_im = mag * jnp.sin(dt * a_im)
    den = a_re * a_re + a_im * a_im
    q_re = ((ab_re - 1.0) * a_re + ab_im * a_im) / den
    q_im = (ab_im * a_re - (ab_re - 1.0) * a_im) / den
    abre_ref[...] = ab_re
    abim_ref[...] = ab_im
    for g in range(S5_GROUPS):
        rs = slice(g * S5_GROUP, (g + 1) * S5_GROUP)
        b_re = bre_ref[rs, :]
        b_im = bim_ref[rs, :]
        bbre_ref[rs, :] = q_re[g:g + 1] * b_re - q_im[g:g + 1] * b_im
        bbim_ref[rs, :] = q_re[g:g + 1] * b_im + q_im[g:g + 1] * b_re


def _s5_prep(log_dt, a_re, a_im, b_re, b_im):
    bt_re = jnp.swapaxes(b_re, 1, 2).reshape(S5_WIDTH, S5_STATE)
    bt_im = jnp.swapaxes(b_im, 1, 2).reshape(S5_WIDTH, S5_STATE)
    gn = jax.ShapeDtypeStruct((S5_GROUPS, S5_STATE), F32)
    wn = jax.ShapeDtypeStruct((S5_WIDTH, S5_STATE), F32)
    return pl.pallas_call(_s5_prep_kernel, out_shape=(gn, gn, wn, wn), name="s5_discretise")(
        log_dt.reshape(S5_GROUPS, 1), a_re, a_im, bt_re, bt_im)


def _block_diag_in(bb):
    eye = jnp.eye(S5_GROUPS, dtype=bb.dtype)
    w = bb.reshape(S5_GROUPS, S5_GROUP, 1, S5_STATE) * eye[:, None, :, None]
    return w.reshape(S5_WIDTH, S5_CH)


def _block_diag_out(c):
    eye = jnp.eye(S5_GROUPS, dtype=c.dtype)
    w = jnp.swapaxes(c, 1, 2)[:, :, None, :] * eye[:, None, :, None]
    return w.reshape(S5_CH, S5_WIDTH)


SEG_Z = (0, 1024)
SEG_XBC = (1024, 1536)
SEG_U = (2560, 512)
SEG_HG = (3072, 2048)
SEG_RW = (5120, 1792)
SEG_DT = (6912, DT_PAD)
IN_PACKED = 7040


def _pack_w_in(w_in):
    off = [0]
    for s in IN_SIZES:
        off.append(off[-1] + s)
    z, xbc, dt, u, q, f, i, g, rw = (w_in[:, off[k]:off[k + 1]] for k in range(9))
    dt = jnp.pad(dt, ((0, 0), (0, DT_PAD - dt.shape[1])))
    return jnp.concatenate([z, xbc, u, q, f, i, g, rw, dt], axis=1).astype(BF16)


def _in_kernel(x_ref, nw_ref, w_ref, cw_ref, cb_ref, dtb_ref, mu_ref, conv0_ref, sh0_ref,
               z_ref, xbc_ref, dt_ref, u_ref, hg_ref, rw_ref, convst_ref, shst_ref,
               full_ref, rwfull_ref, *, bsz, tb):
    step = pl.program_id(0)
    lookback = (SSD_CONV - 1) * bsz

    @pl.when(step == 0)
    def _():
        full_ref[0:lookback, :] = conv0_ref[...]
        rwfull_ref[0:bsz, :] = sh0_ref[...]

    xn = _rms(x_ref[...], nw_ref[...]).astype(BF16)

    def proj(seg):
        return jnp.dot(xn, w_ref[:, seg[0]:seg[0] + seg[1]], preferred_element_type=F32)

    z_ref[...] = proj(SEG_Z)
    u_ref[...] = proj(SEG_U)
    hg_ref[...] = proj(SEG_HG)
    dt_ref[...] = jax.nn.softplus(proj(SEG_DT) + dtb_ref[...])

    full_ref[lookback:lookback + tb, :] = proj(SEG_XBC)
    acc = cb_ref[...] + full_ref[0:tb, :] * cw_ref[0:1, :]
    for j in range(1, SSD_CONV):
        acc = acc + full_ref[j * bsz:j * bsz + tb, :] * cw_ref[j:j + 1, :]
    xbc_ref[...] = _silu(acc)
    _shift_rows_down(full_ref, tb, lookback)
    convst_ref[...] = full_ref[0:lookback, :]

    rwfull_ref[bsz:bsz + tb, :] = proj(SEG_RW)
    cur = rwfull_ref[bsz:bsz + tb, :]
    prev = rwfull_ref[0:tb, :]
    rw_ref[...] = cur + (prev - cur) * mu_ref[...]
    last = rwfull_ref[tb:tb + bsz, :]
    rwfull_ref[0:bsz, :] = last
    shst_ref[...] = last


def _in_proj(x, bsz, tb, norm_w, w_packed, conv_w, conv_b, dt_bias, mu, conv0, shift0):
    rows = x.shape[0]
    lookback = (SSD_CONV - 1) * bsz
    assert rows % tb == 0 and tb % bsz == 0
    row = lambda w: pl.BlockSpec((tb, w), lambda i: (i, 0))
    widths = (1024, SSD_CONV_CH, DT_PAD, S5_WIDTH, 4 * HG_WIDTH, RW_PROJ)
    out_shape = tuple(jax.ShapeDtypeStruct((rows, w), F32) for w in widths) + (
        jax.ShapeDtypeStruct((lookback, SSD_CONV_CH), F32), jax.ShapeDtypeStruct((bsz, RW_PROJ), F32))
    out_specs = tuple(row(w) for w in widths) + (_const_spec((lookback, SSD_CONV_CH)), _const_spec((bsz, RW_PROJ)))
    in_specs = [row(D_MODEL), _const_in((1, D_MODEL)), _const_in((D_MODEL, IN_PACKED)),
                _const_in((SSD_CONV, SSD_CONV_CH)), _const_in((1, SSD_CONV_CH)), _const_in((1, DT_PAD)),
                _const_in((1, RW_PROJ)), _const_in((lookback, SSD_CONV_CH)), _const_in((bsz, RW_PROJ))]
    return pl.pallas_call(
        functools.partial(_in_kernel, bsz=bsz, tb=tb),
        grid=(rows // tb,), in_specs=in_specs, out_specs=out_specs, out_shape=out_shape,
        scratch_shapes=[pltpu.VMEM((lookback + tb, SSD_CONV_CH), F32), pltpu.VMEM((bsz + tb, RW_PROJ), F32)],
        compiler_params=_params(("arbitrary",)), name="in_proj",
    )(x, norm_w, w_packed, conv_w, conv_b, dt_bias, mu, conv0, shift0)


def _ssd_kernel(z_ref, xbc_ref, dt_ref, s0_ref, alog_ref, d_ref, nw_ref, y_ref, s_ref,
                z_dense, xbc_dense, dt_dense, *, lseg, nb):
    _init_states_transposed(s_ref, s0_ref, SSD_HEADS)

    rows = lseg * nb
    sh = int(math.log2(lseg))
    n_x = SSD_HEADS * SSD_HEADDIM
    gw = SSD_GROUPS * SSD_D_STATE
    hpg = SSD_HEADS // SSD_GROUPS
    neg_a = -jnp.exp(alog_ref[...])
    d_skip = d_ref[...]
    i, j = _iotas(rows)
    same = (i >> sh) == (j >> sh)
    tril = same & (j <= i)
    m_tril = tril.astype(BF16)
    m_triu = (same & (i <= j)).astype(BF16)
    m_same = same.astype(BF16)
    seg = lambda t, s: t[s * lseg:(s + 1) * lseg]
    cat = lambda parts: parts[0] if nb == 1 else jnp.concatenate(parts, axis=0)

    col_seg = [(j[0:1, :] >> sh) == s for s in range(nb)]

    def problem(p):
        z = _load_rows(z_ref, p, nb, z_dense)
        xbc = _load_rows(xbc_ref, p, nb, xbc_dense)
        dt = _load_rows(dt_ref, p, nb, dt_dense)
        states = [[s_ref[p * nb + s, h] for s in range(nb)] for h in range(SSD_HEADS)]
        a = dt * neg_a
        cum = _mask_dot(m_tril, a, 2)
        cum_t = _dot_mask(a.T, m_triu, 2)
        tot = _mask_dot(m_same, a, 2)
        dec_end = jnp.exp(tot - cum)
        dec_in = jnp.exp(cum)
        dec_tot = jnp.exp(tot)
        bm_t = xbc[:, n_x:n_x + gw].T.astype(BF16)
        bts = [bm_t[g * SSD_D_STATE:(g + 1) * SSD_D_STATE] for g in range(SSD_GROUPS)]
        if nb > 1:
            zero = jnp.zeros_like(bts[0])
            bts_seg = [[jnp.where(col_seg[s], bts[g], zero) for s in range(nb)] for g in range(SSD_GROUPS)]
        else:
            bts_seg = [[bts[g]] for g in range(SSD_GROUPS)]
        cms = [xbc[:, n_x + gw + g * SSD_D_STATE:n_x + gw + (g + 1) * SSD_D_STATE].astype(BF16)
               for g in range(SSD_GROUPS)]
        gmats = [_bdot(cms[g], bts[g]) for g in range(SSD_GROUPS)]
        ys, new_states = [], []
        for h in range(SSD_HEADS):
            g = h // hpg
            cm = cms[g]
            xh = xbc[:, h * SSD_HEADDIM:(h + 1) * SSD_HEADDIM]
            xdt = xh * dt[:, h:h + 1]
            diff = cum[:, h:h + 1] - cum_t[h:h + 1, :]
            lmat = jnp.where(tril, jnp.exp(jnp.minimum(diff, 0.0)), 0.0)
            y = _bdot(gmats[g] * lmat, xdt)
            xdec = (xdt * dec_end[:, h:h + 1]).astype(BF16)
            y_off = cat([_bdot(seg(cm, s), states[h][s]) for s in range(nb)]) * dec_in[:, h:h + 1]
            new_states.append([states[h][s] * dec_tot[s * lseg:s * lseg + 1, h:h + 1]
                               + _bdot(bts_seg[g][s], xdec) for s in range(nb)])
            ys.append(y + y_off + xh * d_skip[:, h:h + 1])
        yall = jnp.concatenate(ys, axis=1)
        yall = _rms(yall * _silu(z), nw_ref[...])
        _store_rows(y_ref, p, yall, nb, lseg)
        for h in range(SSD_HEADS):
            for s in range(nb):
                s_ref[p * nb + s, h] = new_states[h][s]

    _for_each_problem(nb, problem)
    _finish_states_transposed(s_ref, SSD_HEADS)


def _mixer_specs(lseg, widths_in, widths_out, state_shape):
    blk = lambda w: pl.BlockSpec((lseg, SUB, w), lambda b, c: (c, b, 0))
    st = pl.BlockSpec((SUB,) + state_shape, lambda b, c: (b,) + (0,) * len(state_shape))
    return [blk(w) for w in widths_in], [blk(w) for w in widths_out], st


def _ssd(z, xbc, dt, s0, a_log, d_skip, norm_w, bsz, seqlen, lseg, nb):
    n_x = SSD_HEADS * SSD_HEADDIM
    view = lambda t, w: t.reshape(seqlen, bsz, w)
    ins, outs, st = _mixer_specs(lseg, (n_x, SSD_CONV_CH, DT_PAD), (n_x,), s0.shape[1:])
    y, s = pl.pallas_call(
        functools.partial(_ssd_kernel, lseg=lseg, nb=nb),
        grid=(bsz // SUB, seqlen // lseg),
        in_specs=ins + [st, _const_in((1, DT_PAD)), _const_in((1, DT_PAD)), _const_in((1, n_x))],
        out_specs=(outs[0], st),
        out_shape=(jax.ShapeDtypeStruct((seqlen, bsz, n_x), F32), jax.ShapeDtypeStruct(s0.shape, F32)),
        scratch_shapes=[pltpu.VMEM((nb * lseg, w), F32) for w in (n_x, SSD_CONV_CH, DT_PAD)],
        compiler_params=_params(("arbitrary", "arbitrary")), name="ssd_mixer",
    )(view(z, n_x), view(xbc, SSD_CONV_CH), view(dt, DT_PAD), s0, a_log, d_skip, norm_w)
    return y.reshape(seqlen * bsz, n_x), s


def _s5_kernel(u_ref, win_ref, wcre_ref, wcim_ref, abre_ref, abim_ref, d_ref, gw_ref, gb_ref, hr0_ref, hi0_ref,
               y_ref, hr_ref, hi_ref, bu_ref, *, bsz, tt):
    @pl.when(pl.program_id(0) == 0)
    def _():
        hr_ref[...] = hr0_ref[...]
        hi_ref[...] = hi0_ref[...]

    u = u_ref[...]
    bu_ref[...] = jnp.dot(u.astype(BF16), win_ref[...], preferred_element_type=F32)
    ab_re = abre_ref[...]
    ab_im = abim_ref[...]

    def step(t, carry):
        r0 = pl.multiple_of(t * bsz, bsz)
        hr = hr_ref[...]
        hi = hi_ref[...]
        nr = ab_re * hr - ab_im * hi + bu_ref[pl.ds(r0, bsz), 0:S5_CH]
        ni = ab_re * hi + ab_im * hr + bu_ref[pl.ds(r0, bsz), S5_CH:2 * S5_CH]
        hr_ref[...] = nr
        hi_ref[...] = ni
        bu_ref[pl.ds(r0, bsz), 0:S5_CH] = nr
        bu_ref[pl.ds(r0, bsz), S5_CH:2 * S5_CH] = ni
        return carry

    lax.fori_loop(0, tt, step, 0)
    y = (jnp.dot(bu_ref[:, 0:S5_CH].astype(BF16), wcre_ref[...], preferred_element_type=F32)
         - jnp.dot(bu_ref[:, S5_CH:2 * S5_CH].astype(BF16), wcim_ref[...], preferred_element_type=F32))
    y = jax.nn.gelu(y + d_ref[...] * u)
    y_ref[...] = y * _sigmoid(jnp.dot(y.astype(BF16), gw_ref[...], preferred_element_type=F32) + gb_ref[...])


def _s5(u, bsz, tt, w_in_bd, w_cre_bd, w_cim_bd, ab_re, ab_im, d_skip, glu_w, glu_b, hr0, hi0):
    rows = u.shape[0]
    tb = tt * bsz
    assert rows % tb == 0
    row = pl.BlockSpec((tb, S5_WIDTH), lambda i: (i, 0))
    st = _const_spec((bsz, S5_CH))
    st_in = _const_in((bsz, S5_CH))
    return pl.pallas_call(
        functools.partial(_s5_kernel, bsz=bsz, tt=tt),
        grid=(rows // tb,),
        in_specs=[row, _const_in((S5_WIDTH, 2 * S5_CH)), _const_in((S5_CH, S5_WIDTH)),
                  _const_in((S5_CH, S5_WIDTH)), _const_in((1, S5_CH)), _const_in((1, S5_CH)),
                  _const_in((1, S5_WIDTH)), _const_in((S5_WIDTH, S5_WIDTH)), _const_in((1, S5_WIDTH)), st_in, st_in],
        out_specs=(row, st, st),
        out_shape=(jax.ShapeDtypeStruct((rows, S5_WIDTH), F32), jax.ShapeDtypeStruct((bsz, S5_CH), F32),
                   jax.ShapeDtypeStruct((bsz, S5_CH), F32)),
        scratch_shapes=[pltpu.VMEM((tb, 2 * S5_CH), F32)],
        compiler_params=_params(("arbitrary",)), name="s5_mixer",
    )(u, w_in_bd, w_cre_bd, w_cim_bd, ab_re, ab_im, d_skip, glu_w, glu_b, hr0, hi0)


def _hg_kernel(x_ref, s0_ref, lb_ref, nw_ref, y_ref, s_ref, x_dense, *, lseg, nb):
    @pl.when(pl.program_id(1) == 0)
    def _():
        s_ref[...] = s0_ref[...]

    rows = lseg * nb
    nlev = int(math.log2(lseg))
    lb = lb_ref[...]
    i, j = _iotas(rows)
    hd = lambda t, h: t[:, h * HG_HEADDIM:(h + 1) * HG_HEADDIM]
    seg = lambda t, s: t[s * lseg:(s + 1) * lseg]
    cat = lambda parts: parts[0] if nb == 1 else jnp.concatenate(parts, axis=0)

    lower, after, pair = [], [], []
    for l in range(nlev + 1):
        bi = i >> l
        bj = j >> l
        lower.append(((bi == bj) & (j <= i)).astype(BF16))
        after.append(((bi == bj) & (i > j)).astype(BF16))
        pair.append((bi == bj + 1) & ((bi & 1) == 1))
    diag_mask = i == j
    col_seg = [(j[0:1, :] >> nlev) == s for s in range(nb)]
    row_id = lax.broadcasted_iota(jnp.int32, (rows, HG_HEADDIM), 0)
    seg_ones = [((row_id >> nlev) == s).astype(BF16) for s in range(nb)]
    rows_of = lambda t, h: t[h * HG_HEADDIM:(h + 1) * HG_HEADDIM]

    def problem(p):
        x = _load_rows(x_ref, p, nb, x_dense)
        states = [[s_ref[p * nb + s, h] for s in range(nb)] for h in range(HG_HEADS)]
        q = _silu(x[:, 0:HG_WIDTH])
        f = lb + (1.0 - lb) * _sigmoid(x[:, HG_WIDTH:2 * HG_WIDTH])
        logf = jnp.log(f)
        k = 1.0 - f
        v = x[:, 2 * HG_WIDTH:3 * HG_WIDTH].astype(BF16)
        gate = _sigmoid(x[:, 3 * HG_WIDTH:4 * HG_WIDTH])
        k_t = k.T
        logf_t = logf.T
        logf_parts = _split(logf, 2)
        logf_t_parts = _split(logf_t, 2)
        msum = lambda m: (jnp.dot(m, logf_parts[0], preferred_element_type=F32)
                          + jnp.dot(m, logf_parts[1], preferred_element_type=F32))
        msum_t = lambda m: (jnp.dot(logf_t_parts[0], m, preferred_element_type=F32)
                            + jnp.dot(logf_t_parts[1], m, preferred_element_type=F32))

        att = [jnp.where(diag_mask, jnp.sum(hd(q, h) * hd(k, h), axis=-1, keepdims=True), 0.0)
               for h in range(HG_HEADS)]
        for l in range(nlev):
            if l == 0:
                qe, ke_t = (q * f).astype(BF16), k_t.astype(BF16)
            else:
                qe = (q * jnp.exp(msum(lower[l]))).astype(BF16)
                ke_t = (k_t * jnp.exp(msum_t(after[l]))).astype(BF16)
            for h in range(HG_HEADS):
                att[h] = att[h] + jnp.where(pair[l], _bdot(hd(qe, h), rows_of(ke_t, h)), 0.0)

        qin = (q * jnp.exp(msum(lower[nlev]))).astype(BF16)
        kend_t = (k_t * jnp.exp(msum_t(after[nlev]))).astype(BF16)
        outs, new_states = [], []
        for h in range(HG_HEADS):
            vh = hd(v, h)
            o = _bdot(att[h], vh) + cat([_bdot(seg(hd(qin, h), s), states[h][s]) for s in range(nb)])
            ns = []
            for s in range(nb):
                tot = (jnp.dot(rows_of(logf_t_parts[0], h), seg_ones[s], preferred_element_type=F32)
                       + jnp.dot(rows_of(logf_t_parts[1], h), seg_ones[s], preferred_element_type=F32))
                kend_s = rows_of(kend_t, h) if nb == 1 else jnp.where(col_seg[s], rows_of(kend_t, h), 0)
                ns.append(jnp.exp(tot) * states[h][s] + _bdot(kend_s, vh))
            new_states.append(ns)
            outs.append(_rms(o, nw_ref[...]) * hd(gate, h))
        _store_rows(y_ref, p, jnp.concatenate(outs, axis=1), nb, lseg)
        for h in range(HG_HEADS):
            for s in range(nb):
                s_ref[p * nb + s, h] = new_states[h][s]

    _for_each_problem(nb, problem)


def _hgrn(x, s0, lb, norm_w, bsz, seqlen, lseg, nb):
    w_in = 4 * HG_WIDTH
    ins, outs, st = _mixer_specs(lseg, (w_in,), (HG_WIDTH,), s0.shape[1:])
    y, s = pl.pallas_call(
        functools.partial(_hg_kernel, lseg=lseg, nb=nb),
        grid=(bsz // SUB, seqlen // lseg),
        in_specs=ins + [st, _const_in((1, HG_WIDTH)), _const_in((1, HG_HEADDIM))],
        out_specs=(outs[0], st),
        out_shape=(jax.ShapeDtypeStruct((seqlen, bsz, HG_WIDTH), F32), jax.ShapeDtypeStruct(s0.shape, F32)),
        scratch_shapes=[pltpu.VMEM((nb * lseg, w_in), F32)],
        compiler_params=_params(("arbitrary", "arbitrary")), name="hgrn_mixer",
    )(x.reshape(seqlen, bsz, w_in), s0, lb, norm_w)
    return y.reshape(seqlen * bsz, HG_WIDTH), s


def _rw_kernel(x_ref, s0_ref, w0_ref, wup_ref, a0_ref, aup_ref, gup_ref, kk_ref, ka_ref, rk_ref, lnw_ref, lnb_ref,
               y_ref, s_ref, x_dense, *, lseg, nb):
    _init_states_transposed(s_ref, s0_ref, RW_HEADS)

    rows = lseg * nb
    nlev = int(math.log2(lseg))
    W = RW_WIDTH
    NH = RW_HEADS
    i, j = _iotas(rows)
    same = (i >> nlev) == (j >> nlev)
    incl = same & (j <= i)
    strict = same & (j < i)
    m_incl = incl.astype(BF16)
    m_same = same.astype(BF16)
    eye = (i == j).astype(F32)
    pair = [((i >> l) == (j >> l) + 1) & (((i >> l) & 1) == 1) for l in range(nlev)]
    li = lax.broadcasted_iota(jnp.int32, (W, W), 0) >> 6
    lj = lax.broadcasted_iota(jnp.int32, (W, W), 1) >> 6
    m_head = (li == lj).astype(BF16)
    head_sum = lambda t: _dot_mask(t, m_head, 2)
    hd = lambda t, h: t[:, h * RW_HEADDIM:(h + 1) * RW_HEADDIM]
    rows_of = lambda t, h: t[h * RW_HEADDIM:(h + 1) * RW_HEADDIM]
    seg = lambda t, s: t[s * lseg:(s + 1) * lseg]
    cat = lambda parts: parts[0] if nb == 1 else jnp.concatenate(parts, axis=0)
    m_upto = (same & (i <= j)).astype(BF16)
    col_seg = [(j[0:1, :] >> nlev) == s for s in range(nb)]
    row_id = lax.broadcasted_iota(jnp.int32, (rows, RW_HEADDIM), 0)
    seg_ones = [((row_id >> nlev) == s).astype(BF16) for s in range(nb)]
    seg_cols = lambda t, s: t if nb == 1 else jnp.where(col_seg[s], t, 0)

    def problem(p):
        x = _load_rows(x_ref, p, nb, x_dense)
        states = [[s_ref[p * nb + s, h] for s in range(nb)] for h in range(NH)]
        r, k, v = x[:, 0:W], x[:, W:2 * W], x[:, 2 * W:3 * W]
        wd, ad, gd = x[:, 3 * W:3 * W + 64], x[:, 3 * W + 64:3 * W + 128], x[:, 3 * W + 128:3 * W + 256]
        w = -jax.nn.softplus(-(w0_ref[...] + _bdot(jnp.tanh(wd), wup_ref[...]))) - 0.5
        logw = -jnp.exp(w)
        ag = _sigmoid(a0_ref[...] + _bdot(ad, aup_ref[...]))
        g = _bdot(_sigmoid(gd), gup_ref[...])
        b_in = _mask_dot(m_incl, logw, 2)
        e_in = jnp.exp(b_in)
        e_ex = jnp.exp(b_in - logw)

        kk = k * kk_ref[...]
        kk = kk * lax.rsqrt(jnp.maximum(head_sum(kk * kk), 1e-24))
        k2 = k * (1.0 + (ag - 1.0) * ka_ref[...])
        kb = kk * ag
        a_t = (-kk * e_ex).astype(BF16)
        r_t = (r * e_in).astype(BF16)
        vb = v.astype(BF16)
        bonus = head_sum(r * k2 * rk_ref[...]) * v

        logw_t_parts = _split(logw.T, 2)
        tsum = lambda parts, m: (jnp.dot(parts[0], m, preferred_element_type=F32)
                                 + jnp.dot(parts[1], m, preferred_element_type=F32))
        b_in_t = tsum(logw_t_parts, m_upto)
        e_neg_t = jnp.exp(-b_in_t)
        e_end_t = jnp.exp(tsum(logw_t_parts, m_same) - b_in_t)
        kb_t = kb.T
        k2_t = k2.T
        b_n = (kb_t * e_neg_t).astype(BF16)
        k_n = (k2_t * e_neg_t).astype(BF16)
        b_e = (kb_t * e_end_t).astype(BF16)
        k_e = (k2_t * e_end_t).astype(BF16)

        a_ab = [jnp.where(strict, _bdot(hd(a_t, h), rows_of(b_n, h)), 0.0) for h in range(NH)]
        a_ak = [jnp.where(strict, _bdot(hd(a_t, h), rows_of(k_n, h)), 0.0) for h in range(NH)]
        a_rb = [jnp.where(incl, _bdot(hd(r_t, h), rows_of(b_n, h)), 0.0) for h in range(NH)]
        a_rk = [jnp.where(incl, _bdot(hd(r_t, h), rows_of(k_n, h)), 0.0) for h in range(NH)]
        zmat = [cat([_bdot(seg(hd(a_t, h), s), states[h][s]) for s in range(nb)]) + _bdot(a_ak[h], hd(vb, h))
                for h in range(NH)]
        y0 = [cat([_bdot(seg(hd(r_t, h), s), states[h][s]) for s in range(nb)]) + _bdot(a_rk[h], hd(vb, h))
              for h in range(NH)]
        inv = [eye + jnp.where(pair[0], a_ab[h], 0.0) for h in range(NH)]
        for l in range(1, nlev):
            tmp = [_bdot(jnp.where(pair[l], a_ab[h], 0.0), inv[h]) for h in range(NH)]
            inv = [inv[h] + _bdot(inv[h], tmp[h]) for h in range(NH)]
        sa = [_bdot(inv[h], zmat[h]) for h in range(NH)]
        ys = [y0[h] + _bdot(a_rb[h], sa[h]) for h in range(NH)]
        e_tot = [[jnp.exp(jnp.dot(rows_of(logw_t_parts[0], h), seg_ones[s], preferred_element_type=F32)
                          + jnp.dot(rows_of(logw_t_parts[1], h), seg_ones[s], preferred_element_type=F32))
                  for s in range(nb)] for h in range(NH)]
        new_states = [[states[h][s] * e_tot[h][s]
                       + _bdot(seg_cols(rows_of(b_e, h), s), sa[h]) + _bdot(seg_cols(rows_of(k_e, h), s), hd(vb, h))
                       for s in range(nb)] for h in range(NH)]
        y = jnp.concatenate(ys, axis=1)
        mu = head_sum(y) * (1.0 / RW_HEADDIM)
        yc = y - mu
        var = head_sum(yc * yc) * (1.0 / RW_HEADDIM)
        y = yc * lax.rsqrt(var + RW_LN_EPS) * lnw_ref[...] + lnb_ref[...] + bonus
        _store_rows(y_ref, p, y * g, nb, lseg)
        for h in range(NH):
            for s in range(nb):
                s_ref[p * nb + s, h] = new_states[h][s]

    _for_each_problem(nb, problem)
    _finish_states_transposed(s_ref, RW_HEADS)


def _rwkv(x, s0, p, bsz, seqlen, lseg, nb):
    ins, outs, st = _mixer_specs(lseg, (RW_PROJ,), (RW_WIDTH,), s0.shape[1:])
    vec = _const_in((1, RW_WIDTH))
    y, s = pl.pallas_call(
        functools.partial(_rw_kernel, lseg=lseg, nb=nb),
        grid=(bsz // SUB, seqlen // lseg),
        in_specs=ins + [st, vec, _const_in((64, RW_WIDTH)), vec, _const_in((64, RW_WIDTH)), _const_in((128, RW_WIDTH)),
                        vec, vec, vec, vec, vec],
        out_specs=(outs[0], st),
        out_shape=(jax.ShapeDtypeStruct((seqlen, bsz, RW_WIDTH), F32), jax.ShapeDtypeStruct(s0.shape, F32)),
        scratch_shapes=[pltpu.VMEM((nb * lseg, RW_PROJ), F32)],
        compiler_params=_params(("arbitrary", "arbitrary")), name="rwkv_mixer",
    )(x.reshape(seqlen, bsz, RW_PROJ), s0, p['w0'], p['w_up'], p['a0'], p['a_up'], p['g_up'],
      p['k_k'], p['k_a'], p['r_k'], p['ln_w'], p['ln_b'])
    return y.reshape(seqlen * bsz, RW_WIDTH), s


def _merge_kernel(x_ref, yssd_ref, ys5_ref, yhg_ref, yrw_ref, nw_ref, wm_ref, bm_ref,
                  wssd_ref, ws5_ref, whg_ref, wrw_ref, wout_ref, o_ref):
    x = x_ref[...]
    xn = _rms(x, nw_ref[...]).astype(BF16)
    merged = None
    for b, (y_ref, w_ref) in enumerate(((yssd_ref, wssd_ref), (ys5_ref, ws5_ref), (yhg_ref, whg_ref), (yrw_ref, wrw_ref))):
        cs = slice(b * D_MODEL, (b + 1) * D_MODEL)
        gate = _sigmoid(jnp.dot(xn, wm_ref[:, cs], preferred_element_type=F32) + bm_ref[:, cs])
        t = gate * jnp.dot(y_ref[...].astype(BF16), w_ref[...], preferred_element_type=F32)
        merged = t if merged is None else merged + t
    o_ref[...] = x + jnp.dot(merged.astype(BF16), wout_ref[...], preferred_element_type=F32)


def _merge(x, y_ssd, y_s5, y_hg, y_rw, tb, norm_w, w_merge, b_merge, w_ssd, w_s5, w_hg, w_rw, w_out):
    rows = x.shape[0]
    row = lambda w: pl.BlockSpec((tb, w), lambda i: (i, 0))
    return pl.pallas_call(
        _merge_kernel, grid=(rows // tb,),
        in_specs=[row(D_MODEL), row(1024), row(S5_WIDTH), row(HG_WIDTH), row(RW_WIDTH),
                  _const_in((1, D_MODEL)), _const_in((D_MODEL, 4 * D_MODEL)), _const_in((1, 4 * D_MODEL)),
                  _const_in((1024, D_MODEL)), _const_in((S5_WIDTH, D_MODEL)), _const_in((HG_WIDTH, D_MODEL)),
                  _const_in((RW_WIDTH, D_MODEL)), _const_in((D_MODEL, D_MODEL))],
        out_specs=row(D_MODEL), out_shape=jax.ShapeDtypeStruct((rows, D_MODEL), F32),
        compiler_params=_params(("arbitrary",)), name="merge",
    )(x, y_ssd, y_s5, y_hg, y_rw, norm_w, w_merge, b_merge, w_ssd, w_s5, w_hg, w_rw, w_out)


def _ffn_kernel(x_ref, nw_ref, wup_ref, cw_ref, cb_ref, wdn_ref, conv0_ref, fnw_ref, o_ref, convst_ref, full_ref,
                *, bsz, tb, final_norm):
    lookback = (FFN_CONV - 1) * bsz

    @pl.when(pl.program_id(0) == 0)
    def _():
        full_ref[0:lookback, :] = conv0_ref[...]

    x = x_ref[...]
    xn = _rms(x, nw_ref[...]).astype(BF16)
    full_ref[lookback:lookback + tb, :] = jnp.dot(xn, wup_ref[...], preferred_element_type=F32)
    acc = cb_ref[...] + full_ref[0:tb, :] * cw_ref[0:1, :]
    for j in range(1, FFN_CONV):
        acc = acc + full_ref[j * bsz:j * bsz + tb, :] * cw_ref[j:j + 1, :]
    _shift_rows_down(full_ref, tb, lookback)
    convst_ref[...] = full_ref[0:lookback, :]
    hidden = jax.nn.gelu(acc[:, 0:D_FF]) * acc[:, D_FF:2 * D_FF]
    out = x + jnp.dot(hidden.astype(BF16), wdn_ref[...], preferred_element_type=F32)
    if final_norm:
        out = _rms(out, fnw_ref[...])
    o_ref[...] = out


def _ffn(x, bsz, tb, norm_w, w_up, conv_w, conv_b, w_down, conv0, final_w, final_norm):
    rows = x.shape[0]
    lookback = (FFN_CONV - 1) * bsz
    assert rows % tb == 0 and tb % bsz == 0
    row = pl.BlockSpec((tb, D_MODEL), lambda i: (i, 0))
    return pl.pallas_call(
        functools.partial(_ffn_kernel, bsz=bsz, tb=tb, final_norm=final_norm),
        grid=(rows // tb,),
        in_specs=[row, _const_in((1, D_MODEL)), _const_in((D_MODEL, 2 * D_FF)), _const_in((FFN_CONV, 2 * D_FF)),
                  _const_in((1, 2 * D_FF)), _const_in((D_FF, D_MODEL)), _const_in((lookback, 2 * D_FF)),
                  _const_in((1, D_MODEL))],
        out_specs=(row, _const_spec((lookback, 2 * D_FF))),
        out_shape=(jax.ShapeDtypeStruct((rows, D_MODEL), F32), jax.ShapeDtypeStruct((lookback, 2 * D_FF), F32)),
        scratch_shapes=[pltpu.VMEM((lookback + tb, 2 * D_FF), F32)],
        compiler_params=_params(("arbitrary",)), name="conv_ffn",
    )(x, norm_w, w_up, conv_w, conv_b, w_down, conv0, final_w)


def _pad_lanes(v, width=DT_PAD):
    return jnp.pad(v, (0, width - v.shape[0])).reshape(1, width)


def _layer_params(l, P, lb_all):
    p = {n: a[l] for n, a in P.items()}
    row = lambda a: a.reshape(1, -1)
    ab_re, ab_im, bb_re, bb_im = _s5_prep(p['s5_log_dt'], p['s5_a_re'], p['s5_a_im'], p['s5_b_re'], p['s5_b_im'])
    q = dict(
        norm1_w=row(p['norm1_w']), w_in=_pack_w_in(p['w_in']),
        ssd_conv_w=p['ssd_conv_w'], ssd_conv_b=row(p['ssd_conv_b']), ssd_dt_bias=_pad_lanes(p['ssd_dt_bias']),
        ssd_a_log=_pad_lanes(p['ssd_a_log']), ssd_d=_pad_lanes(p['ssd_d']), ssd_norm_w=row(p['ssd_norm_w']),
        s5_win=jnp.concatenate([_block_diag_in(bb_re), _block_diag_in(bb_im)], axis=1).astype(BF16),
        s5_wcre=_block_diag_out(p['s5_c_re']).astype(BF16), s5_wcim=_block_diag_out(p['s5_c_im']).astype(BF16),
        s5_ab_re=row(ab_re), s5_ab_im=row(ab_im), s5_d=row(p['s5_d']),
        s5_glu_w=p['s5_glu_w'].astype(BF16), s5_glu_b=row(p['s5_glu_b']),
        hg_lb=lb_all[l:l + 1], hg_norm_w=row(p['hg_norm_w']),
        rw_mu=row(p['rw_mu']),
        rw=dict(w0=row(p['rw_w0']), w_up=p['rw_w_up'].astype(BF16), a0=row(p['rw_a0']), a_up=p['rw_a_up'].astype(BF16),
                g_up=p['rw_g_up'].astype(BF16), k_k=row(p['rw_k_k']), k_a=row(p['rw_k_a']), r_k=row(p['rw_r_k']),
                ln_w=row(p['rw_ln_w']), ln_b=row(p['rw_ln_b'])),
        w_merge=p['w_merge'].astype(BF16), b_merge=row(p['b_merge']),
        w_br_ssd=p['w_br_ssd'].astype(BF16), w_br_s5=p['w_br_s5'].astype(BF16), w_br_hg=p['w_br_hg'].astype(BF16),
        w_br_rw=p['w_br_rw'].astype(BF16), w_out=p['w_out'].astype(BF16),
        norm2_w=row(p['norm2_w']), ffn_up=p['ffn_up'].astype(BF16), ffn_conv_w=p['ffn_conv_w'],
        ffn_conv_b=row(p['ffn_conv_b']), ffn_down=p['ffn_down'].astype(BF16),
    )
    return q


class _Group:
    def __init__(self, bsz, seqlen):
        self.bsz, self.seqlen = bsz, seqlen
        rows = bsz * seqlen
        self.tb = min(rows, max(256, bsz))
        self.tt = self.tb // bsz
        assert bsz % SUB == 0
        if seqlen >= 128:
            self.ssd, self.hg, self.rw = (128, 2), (64, 2), (64, 2)
        else:
            self.ssd = self.hg = self.rw = (seqlen, SUB)


def _time_major(a):
    a = jnp.swapaxes(a, 0, 1)
    return a.reshape((a.shape[0] * a.shape[1],) + a.shape[2:])


def _batch_major(a, bsz):
    a = a.reshape((a.shape[0] // bsz, bsz) + a.shape[1:])
    return jnp.swapaxes(a, 0, 1)


def _trunk(x, states, layers, final_norm_w, grp):
    bsz, seqlen = grp.bsz, grp.seqlen
    xt = _time_major(x)
    new_states = []
    for l, q in enumerate(layers):
        s_ssd, s_conv, s_s5r, s_s5i, s_hg, s_rw, s_shift, s_fconv = states[l]
        z, xbc, dt, u, hg, rwx, conv_new, shift_new = _in_proj(
            xt, bsz, grp.tb, q['norm1_w'], q['w_in'], q['ssd_conv_w'], q['ssd_conv_b'], q['ssd_dt_bias'], q['rw_mu'],
            _time_major(s_conv), s_shift)
        y_ssd, ssd_new = _ssd(z, xbc, dt, s_ssd, q['ssd_a_log'], q['ssd_d'], q['ssd_norm_w'],
                              bsz, seqlen, *grp.ssd)
        y_s5, hr_new, hi_new = _s5(u, bsz, grp.tt, q['s5_win'], q['s5_wcre'], q['s5_wcim'], q['s5_ab_re'], q['s5_ab_im'],
                                   q['s5_d'], q['s5_glu_w'], q['s5_glu_b'],
                                   s_s5r.reshape(bsz, S5_CH), s_s5i.reshape(bsz, S5_CH))
        y_hg, hg_new = _hgrn(hg, s_hg, q['hg_lb'], q['hg_norm_w'], bsz, seqlen, *grp.hg)
        y_rw, rw_new = _rwkv(rwx, s_rw, q['rw'], bsz, seqlen, *grp.rw)
        x1 = _merge(xt, y_ssd, y_s5, y_hg, y_rw, grp.tb, q['norm1_w'], q['w_merge'], q['b_merge'],
                    q['w_br_ssd'], q['w_br_s5'], q['w_br_hg'], q['w_br_rw'], q['w_out'])
        xt, fconv_new = _ffn(x1, bsz, grp.tb, q['norm2_w'], q['ffn_up'], q['ffn_conv_w'], q['ffn_conv_b'],
                             q['ffn_down'], _time_major(s_fconv), final_norm_w.reshape(1, D_MODEL), l == DEPTH - 1)
        new_states.append((ssd_new, _batch_major(conv_new, bsz),
                           hr_new.reshape(bsz, S5_GROUPS, S5_STATE), hi_new.reshape(bsz, S5_GROUPS, S5_STATE),
                           hg_new, rw_new, shift_new, _batch_major(fconv_new, bsz)))
    stacked = tuple(jnp.stack([st[k] for st in new_states], axis=0) for k in range(8))
    return _batch_major(xt, bsz), stacked


def _zero_states(bsz):
    z = lambda *s: jnp.zeros((bsz,) + s, F32)
    return (z(SSD_HEADS, SSD_HEADDIM, SSD_D_STATE), z(SSD_CONV - 1, SSD_CONV_CH), z(S5_GROUPS, S5_STATE),
            z(S5_GROUPS, S5_STATE), z(HG_HEADS, HG_HEADDIM, HG_HEADDIM), z(RW_HEADS, RW_HEADDIM, RW_HEADDIM),
            z(RW_PROJ), z(FFN_CONV - 1, 2 * D_FF))


def kernel(x_prompt, x_sample, state_ssd, state_ssd_conv, state_s5_re, state_s5_im, state_hgrn, state_rwkv, state_rwkv_shift, state_ffn_conv, norm1_w, w_in, ssd_conv_w, ssd_conv_b, ssd_dt_bias, ssd_a_log, ssd_d, ssd_norm_w, s5_a_re, s5_a_im, s5_log_dt, s5_b_re, s5_b_im, s5_c_re, s5_c_im, s5_d, s5_glu_w, s5_glu_b, hg_lb_raw, hg_norm_w, rw_mu, rw_w0, rw_w_up, rw_a0, rw_a_up, rw_g_up, rw_k_k, rw_k_a, rw_r_k, rw_ln_w, rw_ln_b, w_br_ssd, w_br_s5, w_br_hg, w_br_rw, w_merge, b_merge, w_out, norm2_w, ffn_up, ffn_conv_w, ffn_conv_b, ffn_down, final_norm_w):
    P = dict(norm1_w=norm1_w, w_in=w_in, ssd_conv_w=ssd_conv_w, ssd_conv_b=ssd_conv_b, ssd_dt_bias=ssd_dt_bias,
             ssd_a_log=ssd_a_log, ssd_d=ssd_d, ssd_norm_w=ssd_norm_w, s5_a_re=s5_a_re, s5_a_im=s5_a_im,
             s5_log_dt=s5_log_dt, s5_b_re=s5_b_re, s5_b_im=s5_b_im, s5_c_re=s5_c_re, s5_c_im=s5_c_im, s5_d=s5_d,
             s5_glu_w=s5_glu_w, s5_glu_b=s5_glu_b, hg_norm_w=hg_norm_w, rw_mu=rw_mu, rw_w0=rw_w0, rw_w_up=rw_w_up,
             rw_a0=rw_a0, rw_a_up=rw_a_up, rw_g_up=rw_g_up,
             rw_k_k=rw_k_k.reshape(DEPTH, RW_WIDTH), rw_k_a=rw_k_a.reshape(DEPTH, RW_WIDTH),
             rw_r_k=rw_r_k.reshape(DEPTH, RW_WIDTH), rw_ln_w=rw_ln_w.reshape(DEPTH, RW_WIDTH),
             rw_ln_b=rw_ln_b.reshape(DEPTH, RW_WIDTH),
             w_br_ssd=w_br_ssd, w_br_s5=w_br_s5, w_br_hg=w_br_hg, w_br_rw=w_br_rw, w_merge=w_merge, b_merge=b_merge,
             w_out=w_out, norm2_w=norm2_w, ffn_up=ffn_up, ffn_conv_w=ffn_conv_w, ffn_conv_b=ffn_conv_b,
             ffn_down=ffn_down)
    lb_all = _lower_bounds(hg_lb_raw)
    layers = [_layer_params(l, P, lb_all) for l in range(DEPTH)]
    sample_states = (state_ssd, state_ssd_conv, state_s5_re, state_s5_im, state_hgrn, state_rwkv,
                     state_rwkv_shift, state_ffn_conv)
    sample_init = [tuple(s[l] for s in sample_states) for l in range(DEPTH)]
    prompt_init = [_zero_states(x_prompt.shape[0])] * DEPTH
    y_prompt, p_states = _trunk(x_prompt, prompt_init, layers, final_norm_w, _Group(*x_prompt.shape[:2]))
    y_sample, s_states = _trunk(x_sample, sample_init, layers, final_norm_w, _Group(*x_sample.shape[:2]))
    return (y_prompt, y_sample) + p_states + s_states
```

```python
import functools
import math

import jax
import jax.numpy as jnp
from jax import lax
from jax.experimental import pallas as pl
from jax.experimental.pallas import tpu as pltpu

F32 = jnp.float32
BF16 = jnp.bfloat16

D_MODEL = 1024
DEPTH = 4
SSD_HEADS = 16
SSD_HEADDIM = 64
SSD_D_STATE = 64
SSD_GROUPS = 4
SSD_CONV = 4
SSD_CONV_CH = 1536
S5_WIDTH = 512
S5_GROUPS = 32
S5_GROUP = 16
S5_STATE = 64
S5_CH = S5_GROUPS * S5_STATE
HG_WIDTH = 512
HG_HEADS = 4
HG_HEADDIM = 128
RW_WIDTH = 512
RW_HEADS = 8
RW_HEADDIM = 64
RW_PROJ = 1792
RW_LN_EPS = 64e-5
D_FF = 2816
FFN_CONV = 3
EPS = 1e-6
IN_SIZES = (1024, 1536, 16, 512, 512, 512, 512, 512, 1792)
DT_PAD = 128

VMEM_LIMIT = 56 * 1024 * 1024


def _bdot(a, b):
    return jnp.dot(a.astype(BF16), b.astype(BF16), preferred_element_type=F32)


def _split(x, n):
    parts = []
    r = x
    for _ in range(n):
        p = r.astype(BF16)
        parts.append(p)
        r = r - p.astype(F32)
    return parts


def _mask_dot(m, x, n=3):
    out = None
    for p in _split(x, n):
        t = jnp.dot(m, p, preferred_element_type=F32)
        out = t if out is None else out + t
    return out


def _dot_mask(x, m, n=3):
    out = None
    for p in _split(x, n):
        t = jnp.dot(p, m, preferred_element_type=F32)
        out = t if out is None else out + t
    return out


def _rms(x, w):
    return x * lax.rsqrt(jnp.mean(x * x, axis=-1, keepdims=True) + EPS) * w


def _sigmoid(x):
    return jax.nn.sigmoid(x)


def _silu(x):
    return x * jax.nn.sigmoid(x)


def _load_rows(ref, p, nb, dense_ref):
    lseg = ref.shape[0]
    for s in range(nb):
        dense_ref[s * lseg:(s + 1) * lseg, :] = ref[:, p * nb + s, :]
    return dense_ref[...]


def _store_rows(ref, p, y, nb, lseg):
    for s in range(nb):
        ref[:, p * nb + s, :] = y[s * lseg:(s + 1) * lseg, :]


def _init_states_transposed(s_ref, s0_ref, nheads, sub):
    @pl.when(pl.program_id(1) == 0)
    def _():
        def body(b, carry):
            for h in range(nheads):
                s_ref[b, h] = s0_ref[b, h].T
            return carry
        lax.fori_loop(0, sub, body, 0)


def _finish_states_transposed(s_ref, nheads, sub):
    @pl.when(pl.program_id(1) == pl.num_programs(1) - 1)
    def _():
        def body(b, carry):
            for h in range(nheads):
                s_ref[b, h] = s_ref[b, h].T
            return carry
        lax.fori_loop(0, sub, body, 0)


def _for_each_problem(sub, nb, problem):
    if nb == sub:
        problem(0)
    else:
        def body(p, carry):
            problem(p)
            return carry
        lax.fori_loop(0, sub // nb, body, 0)


def _shift_rows_down(ref, dist, n):
    for off in range(0, n, dist):
        m = min(dist, n - off)
        ref[off:off + m, :] = ref[off + dist:off + dist + m, :]


def _iotas(rows):
    i = lax.broadcasted_iota(jnp.int32, (rows, rows), 0)
    j = lax.broadcasted_iota(jnp.int32, (rows, rows), 1)
    return i, j


def _const_spec(shape):
    nd = len(shape)
    return pl.BlockSpec(shape, lambda *_: (0,) * nd)


def _const_in(shape):
    nd = len(shape)
    return pl.BlockSpec(shape, lambda *_: (0,) * nd, pipeline_mode=pl.Buffered(1))


def _params(sem):
    return pltpu.CompilerParams(dimension_semantics=sem, vmem_limit_bytes=VMEM_LIMIT)


def _lb_kernel(raw_ref, o_ref):
    raw = raw_ref[...]
    m = jnp.max(raw, axis=0, keepdims=True)
    e = jnp.exp(raw - m)
    sm = e / jnp.sum(e, axis=0, keepdims=True)
    acc = jnp.zeros_like(sm[0:1])
    rows = []
    for l in range(DEPTH):
        acc = acc + sm[l:l + 1]
        rows.append(acc)
    first = rows[0]
    o_ref[...] = jnp.concatenate([r - first for r in rows], axis=0)


def _lower_bounds(hg_lb_raw):
    return pl.pallas_call(_lb_kernel, out_shape=jax.ShapeDtypeStruct((DEPTH, HG_WIDTH), F32),
                          name="hg_lower_bounds")(hg_lb_raw)


S5_PACK = 8
S5_SLABS = S5_GROUPS // S5_PACK
S5_SLAB_IN = S5_PACK * S5_GROUP
S5_SLAB_CH = S5_PACK * S5_STATE


def _s5_prep_kernel(ldt_ref, are_ref, aim_ref, bre_ref, bim_ref, cre_ref, cim_ref,
                    abre_ref, abim_ref, win_ref, wcre_ref, wcim_ref):
    dt = jnp.exp(ldt_ref[...])
    a_re = are_ref[...]
    a_im = aim_ref[...]
    mag = jnp.exp(dt * a_re)
    ab_re = mag * jnp.cos(dt * a_im)
    ab_im = mag * jnp.sin(dt * a_im)
    den = a_re * a_re + a_im * a_im
    q_re = ((ab_re - 1.0) * a_re + ab_im * a_im) / den
    q_im = (ab_im * a_re - (ab_re - 1.0) * a_im) / den
    abre_ref[...] = ab_re
    abim_ref[...] = ab_im
    win_ref[...] = jnp.zeros(win_ref.shape, win_ref.dtype)
    wcre_ref[...] = jnp.zeros(wcre_ref.shape, wcre_ref.dtype)
    wcim_ref[...] = jnp.zeros(wcim_ref.shape, wcim_ref.dtype)
    for g in range(S5_GROUPS):
        slab, r = divmod(g, S5_PACK)
        rows_in = slice(r * S5_GROUP, (r + 1) * S5_GROUP)
        ch = slice(r * S5_STATE, (r + 1) * S5_STATE)
        ch_im = slice(S5_SLAB_CH + r * S5_STATE, S5_SLAB_CH + (r + 1) * S5_STATE)
        b_re = bre_ref[g * S5_GROUP:(g + 1) * S5_GROUP, :]
        b_im = bim_ref[g * S5_GROUP:(g + 1) * S5_GROUP, :]
        win_ref[slab, rows_in, ch] = (q_re[g:g + 1] * b_re - q_im[g:g + 1] * b_im).astype(win_ref.dtype)
        win_ref[slab, rows_in, ch_im] = (q_re[g:g + 1] * b_im + q_im[g:g + 1] * b_re).astype(win_ref.dtype)
        wcre_ref[slab, ch, rows_in] = cre_ref[g * S5_STATE:(g + 1) * S5_STATE, :].astype(wcre_ref.dtype)
        wcim_ref[slab, ch, rows_in] = cim_ref[g * S5_STATE:(g + 1) * S5_STATE, :].astype(wcim_ref.dtype)


def _s5_prep(log_dt, a_re, a_im, b_re, b_im, c_re, c_im):
    bt_re = jnp.swapaxes(b_re, 1, 2).reshape(S5_WIDTH, S5_STATE)
    bt_im = jnp.swapaxes(b_im, 1, 2).reshape(S5_WIDTH, S5_STATE)
    ct_re = jnp.swapaxes(c_re, 1, 2).reshape(S5_CH, S5_GROUP)
    ct_im = jnp.swapaxes(c_im, 1, 2).reshape(S5_CH, S5_GROUP)
    gn = jax.ShapeDtypeStruct((S5_GROUPS, S5_STATE), F32)
    return pl.pallas_call(
        _s5_prep_kernel,
        out_shape=(gn, gn, jax.ShapeDtypeStruct((S5_SLABS, S5_SLAB_IN, 2 * S5_SLAB_CH), BF16),
                   jax.ShapeDtypeStruct((S5_SLABS, S5_SLAB_CH, S5_SLAB_IN), BF16),
                   jax.ShapeDtypeStruct((S5_SLABS, S5_SLAB_CH, S5_SLAB_IN), BF16)),
        name="s5_discretise")(log_dt.reshape(S5_GROUPS, 1), a_re, a_im, bt_re, bt_im, ct_re, ct_im)


SEG_Z = (0, 1024)
SEG_XBC = (1024, 1536)
SEG_U = (2560, 512)
SEG_HG = (3072, 2048)
SEG_RW = (5120, 1792)
SEG_DT = (6912, DT_PAD)
IN_PACKED = 7040


def _pack_w_in(w_in):
    off = [0]
    for s in IN_SIZES:
        off.append(off[-1] + s)
    z, xbc, dt, u, q, f, i, g, rw = (w_in[:, off[k]:off[k + 1]] for k in range(9))
    dt = jnp.pad(dt, ((0, 0), (0, DT_PAD - dt.shape[1])))
    return jnp.concatenate([z, xbc, u, q, f, i, g, rw, dt], axis=1).astype(BF16)


def _in_kernel(x_ref, nw_ref, w_ref, cw_ref, cb_ref, dtb_ref, mu_ref, conv0_ref, sh0_ref,
               z_ref, xbc_ref, dt_ref, u_ref, hg_ref, rw_ref, convst_ref, shst_ref,
               full_ref, rwfull_ref, *, bsz, tb):
    step = pl.program_id(0)
    lookback = (SSD_CONV - 1) * bsz

    @pl.when(step == 0)
    def _():
        full_ref[0:lookback, :] = conv0_ref[...]
        rwfull_ref[0:bsz, :] = sh0_ref[...]

    xn = _rms(x_ref[...], nw_ref[...]).astype(BF16)

    def proj(seg):
        return jnp.dot(xn, w_ref[:, seg[0]:seg[0] + seg[1]], preferred_element_type=F32)

    z_ref[...] = proj(SEG_Z)
    u_ref[...] = proj(SEG_U)
    hg_ref[...] = proj(SEG_HG)
    dt_ref[...] = jax.nn.softplus(proj(SEG_DT) + dtb_ref[...])

    full_ref[lookback:lookback + tb, :] = proj(SEG_XBC)
    acc = cb_ref[...] + full_ref[0:tb, :] * cw_ref[0:1, :]
    for j in range(1, SSD_CONV):
        acc = acc + full_ref[j * bsz:j * bsz + tb, :] * cw_ref[j:j + 1, :]
    xbc_ref[...] = _silu(acc)
    _shift_rows_down(full_ref, tb, lookback)
    convst_ref[...] = full_ref[0:lookback, :]

    rwfull_ref[bsz:bsz + tb, :] = proj(SEG_RW)
    cur = rwfull_ref[bsz:bsz + tb, :]
    prev = rwfull_ref[0:tb, :]
    rw_ref[...] = cur + (prev - cur) * mu_ref[...]
    last = rwfull_ref[tb:tb + bsz, :]
    rwfull_ref[0:bsz, :] = last
    shst_ref[...] = last


def _in_proj(x, bsz, tb, norm_w, w_packed, conv_w, conv_b, dt_bias, mu, conv0, shift0):
    rows = x.shape[0]
    lookback = (SSD_CONV - 1) * bsz
    assert rows % tb == 0 and tb % bsz == 0
    row = lambda w: pl.BlockSpec((tb, w), lambda i: (i, 0))
    widths = (1024, SSD_CONV_CH, DT_PAD, S5_WIDTH, 4 * HG_WIDTH, RW_PROJ)
    out_shape = tuple(jax.ShapeDtypeStruct((rows, w), F32) for w in widths) + (
        jax.ShapeDtypeStruct((lookback, SSD_CONV_CH), F32), jax.ShapeDtypeStruct((bsz, RW_PROJ), F32))
    out_specs = tuple(row(w) for w in widths) + (_const_spec((lookback, SSD_CONV_CH)), _const_spec((bsz, RW_PROJ)))
    in_specs = [row(D_MODEL), _const_in((1, D_MODEL)), _const_in((D_MODEL, IN_PACKED)),
                _const_in((SSD_CONV, SSD_CONV_CH)), _const_in((1, SSD_CONV_CH)), _const_in((1, DT_PAD)),
                _const_in((1, RW_PROJ)), _const_in((lookback, SSD_CONV_CH)), _const_in((bsz, RW_PROJ))]
    return pl.pallas_call(
        functools.partial(_in_kernel, bsz=bsz, tb=tb),
        grid=(rows // tb,), in_specs=in_specs, out_specs=out_specs, out_shape=out_shape,
        scratch_shapes=[pltpu.VMEM((lookback + tb, SSD_CONV_CH), F32), pltpu.VMEM((bsz + tb, RW_PROJ), F32)],
        compiler_params=_params(("arbitrary",)), name="in_proj",
    )(x, norm_w, w_packed, conv_w, conv_b, dt_bias, mu, conv0, shift0)


def _ssd_kernel(z_ref, xbc_ref, dt_ref, s0_ref, alog_ref, d_ref, nw_ref, y_ref, s_ref,
                z_dense, xbc_dense, dt_dense, *, lseg, nb, sub):
    _init_states_transposed(s_ref, s0_ref, SSD_HEADS, sub)

    rows = lseg * nb
    sh = int(math.log2(lseg))
    n_x = SSD_HEADS * SSD_HEADDIM
    gw = SSD_GROUPS * SSD_D_STATE
    hpg = SSD_HEADS // SSD_GROUPS
    neg_a = -jnp.exp(alog_ref[...])
    d_skip = d_ref[...]
    i, j = _iotas(rows)
    same = (i >> sh) == (j >> sh)
    tril = same & (j <= i)
    m_tril = tril.astype(BF16)
    m_triu = (same & (i <= j)).astype(BF16)
    m_same = same.astype(BF16)
    seg = lambda t, s: t[s * lseg:(s + 1) * lseg]
    cat = lambda parts: parts[0] if nb == 1 else jnp.concatenate(parts, axis=0)

    col_seg = [(j[0:1, :] >> sh) == s for s in range(nb)]

    def problem(p):
        z = _load_rows(z_ref, p, nb, z_dense)
        xbc = _load_rows(xbc_ref, p, nb, xbc_dense)
        dt = _load_rows(dt_ref, p, nb, dt_dense)
        states = [[s_ref[p * nb + s, h] for s in range(nb)] for h in range(SSD_HEADS)]
        a = dt * neg_a
        cum = _mask_dot(m_tril, a, 2)
        cum_t = _dot_mask(a.T, m_triu, 2)
        tot = _mask_dot(m_same, a, 2)
        dec_end = jnp.exp(tot - cum)
        dec_in = jnp.exp(cum)
        dec_tot = jnp.exp(tot)
        bm_t = xbc[:, n_x:n_x + gw].T.astype(BF16)
        bts = [bm_t[g * SSD_D_STATE:(g + 1) * SSD_D_STATE] for g in range(SSD_GROUPS)]
        if nb > 1:
            zero = jnp.zeros_like(bts[0])
            bts_seg = [[jnp.where(col_seg[s], bts[g], zero) for s in range(nb)] for g in range(SSD_GROUPS)]
        else:
            bts_seg = [[bts[g]] for g in range(SSD_GROUPS)]
        cms = [xbc[:, n_x + gw + g * SSD_D_STATE:n_x + gw + (g + 1) * SSD_D_STATE].astype(BF16)
               for g in range(SSD_GROUPS)]
        gmats = [_bdot(cms[g], bts[g]) for g in range(SSD_GROUPS)]
        ys, new_states = [], []
        for h in range(SSD_HEADS):
            g = h // hpg
            cm = cms[g]
            xh = xbc[:, h * SSD_HEADDIM:(h + 1) * SSD_HEADDIM]
            xdt = xh * dt[:, h:h + 1]
            diff = cum[:, h:h + 1] - cum_t[h:h + 1, :]
            lmat = jnp.where(tril, jnp.exp(jnp.minimum(diff, 0.0)), 0.0)
            y = _bdot(gmats[g] * lmat, xdt)
            xdec = (xdt * dec_end[:, h:h + 1]).astype(BF16)
            y_off = cat([_bdot(seg(cm, s), states[h][s]) for s in range(nb)]) * dec_in[:, h:h + 1]
            new_states.append([states[h][s] * dec_tot[s * lseg:s * lseg + 1, h:h + 1]
                               + _bdot(bts_seg[g][s], xdec) for s in range(nb)])
            ys.append(y + y_off + xh * d_skip[:, h:h + 1])
        yall = jnp.concatenate(ys, axis=1)
        yall = _rms(yall * _silu(z), nw_ref[...])
        _store_rows(y_ref, p, yall, nb, lseg)
        for h in range(SSD_HEADS):
            for s in range(nb):
                s_ref[p * nb + s, h] = new_states[h][s]

    _for_each_problem(sub, nb, problem)
    _finish_states_transposed(s_ref, SSD_HEADS, sub)


def _mixer_specs(lseg, sub, widths_in, widths_out, state_shape):
    blk = lambda w: pl.BlockSpec((lseg, sub, w), lambda b, c: (c, b, 0))
    st = pl.BlockSpec((sub,) + state_shape, lambda b, c: (b,) + (0,) * len(state_shape))
    return [blk(w) for w in widths_in], [blk(w) for w in widths_out], st


def _ssd(z, xbc, dt, s0, a_log, d_skip, norm_w, bsz, seqlen, lseg, nb, sub):
    n_x = SSD_HEADS * SSD_HEADDIM
    view = lambda t, w: t.reshape(seqlen, bsz, w)
    ins, outs, st = _mixer_specs(lseg, sub, (n_x, SSD_CONV_CH, DT_PAD), (n_x,), s0.shape[1:])
    y, s = pl.pallas_call(
        functools.partial(_ssd_kernel, lseg=lseg, nb=nb, sub=sub),
        grid=(bsz // sub, seqlen // lseg),
        in_specs=ins + [st, _const_in((1, DT_PAD)), _const_in((1, DT_PAD)), _const_in((1, n_x))],
        out_specs=(outs[0], st),
        out_shape=(jax.ShapeDtypeStruct((seqlen, bsz, n_x), F32), jax.ShapeDtypeStruct(s0.shape, F32)),
        scratch_shapes=[pltpu.VMEM((nb * lseg, w), F32) for w in (n_x, SSD_CONV_CH, DT_PAD)],
        compiler_params=_params(("arbitrary", "arbitrary")), name="ssd_mixer",
    )(view(z, n_x), view(xbc, SSD_CONV_CH), view(dt, DT_PAD), s0, a_log, d_skip, norm_w)
    return y.reshape(seqlen * bsz, n_x), s


def _s5_kernel(u_ref, win_ref, wcre_ref, wcim_ref, abre_ref, abim_ref, d_ref, gw_ref, gb_ref, hr0_ref, hi0_ref,
               y_ref, hr_ref, hi_ref, bu_ref, *, bsz, tt):
    @pl.when(pl.program_id(0) == 0)
    def _():
        hr_ref[...] = hr0_ref[...]
        hi_ref[...] = hi0_ref[...]

    u = u_ref[...]
    ub = u.astype(BF16)
    for q in range(S5_SLABS):
        t = jnp.dot(ub[:, q * S5_SLAB_IN:(q + 1) * S5_SLAB_IN], win_ref[q], preferred_element_type=F32)
        bu_ref[:, q * S5_SLAB_CH:(q + 1) * S5_SLAB_CH] = t[:, 0:S5_SLAB_CH]
        bu_ref[:, S5_CH + q * S5_SLAB_CH:S5_CH + (q + 1) * S5_SLAB_CH] = t[:, S5_SLAB_CH:2 * S5_SLAB_CH]
    ab_re = abre_ref[...]
    ab_im = abim_ref[...]

    def step(t, carry):
        r0 = pl.multiple_of(t * bsz, bsz)
        hr = hr_ref[...]
        hi = hi_ref[...]
        nr = ab_re * hr - ab_im * hi + bu_ref[pl.ds(r0, bsz), 0:S5_CH]
        ni = ab_re * hi + ab_im * hr + bu_ref[pl.ds(r0, bsz), S5_CH:2 * S5_CH]
        hr_ref[...] = nr
        hi_ref[...] = ni
        bu_ref[pl.ds(r0, bsz), 0:S5_CH] = nr
        bu_ref[pl.ds(r0, bsz), S5_CH:2 * S5_CH] = ni
        return carry

    lax.fori_loop(0, tt, step, 0)
    ys = []
    for q in range(S5_SLABS):
        h_re = bu_ref[:, q * S5_SLAB_CH:(q + 1) * S5_SLAB_CH].astype(BF16)
        h_im = bu_ref[:, S5_CH + q * S5_SLAB_CH:S5_CH + (q + 1) * S5_SLAB_CH].astype(BF16)
        ys.append(jnp.dot(h_re, wcre_ref[q], preferred_element_type=F32)
                  - jnp.dot(h_im, wcim_ref[q], preferred_element_type=F32))
    y = jnp.concatenate(ys, axis=1)
    y = jax.nn.gelu(y + d_ref[...] * u)
    y_ref[...] = y * _sigmoid(jnp.dot(y.astype(BF16), gw_ref[...], preferred_element_type=F32) + gb_ref[...])


def _s5(u, bsz, tt, w_in_bd, w_cre_bd, w_cim_bd, ab_re, ab_im, d_skip, glu_w, glu_b, hr0, hi0):
    rows = u.shape[0]
    tb = tt * bsz
    assert rows % tb == 0
    row = pl.BlockSpec((tb, S5_WIDTH), lambda i: (i, 0))
    st = _const_spec((bsz, S5_CH))
    st_in = _const_in((bsz, S5_CH))
    return pl.pallas_call(
        functools.partial(_s5_kernel, bsz=bsz, tt=tt),
        grid=(rows // tb,),
        in_specs=[row, _const_in((S5_SLABS, S5_SLAB_IN, 2 * S5_SLAB_CH)), _const_in((S5_SLABS, S5_SLAB_CH, S5_SLAB_IN)),
                  _const_in((S5_SLABS, S5_SLAB_CH, S5_SLAB_IN)), _const_in((1, S5_CH)), _const_in((1, S5_CH)),
                  _const_in((1, S5_WIDTH)), _const_in((S5_WIDTH, S5_WIDTH)), _const_in((1, S5_WIDTH)), st_in, st_in],
        out_specs=(row, st, st),
        out_shape=(jax.ShapeDtypeStruct((rows, S5_WIDTH), F32), jax.ShapeDtypeStruct((bsz, S5_CH), F32),
                   jax.ShapeDtypeStruct((bsz, S5_CH), F32)),
        scratch_shapes=[pltpu.VMEM((tb, 2 * S5_CH), F32)],
        compiler_params=_params(("arbitrary",)), name="s5_mixer",
    )(u, w_in_bd, w_cre_bd, w_cim_bd, ab_re, ab_im, d_skip, glu_w, glu_b, hr0, hi0)


def _hg_kernel(x_ref, s0_ref, lb_ref, nw_ref, y_ref, s_ref, x_dense, *, lseg, nb, sub):
    @pl.when(pl.program_id(1) == 0)
    def _():
        s_ref[...] = s0_ref[...]

    rows = lseg * nb
    nlev = int(math.log2(lseg))
    lb = lb_ref[...]
    i, j = _iotas(rows)
    hd = lambda t, h: t[:, h * HG_HEADDIM:(h + 1) * HG_HEADDIM]
    seg = lambda t, s: t[s * lseg:(s + 1) * lseg]
    cat = lambda parts: parts[0] if nb == 1 else jnp.concatenate(parts, axis=0)

    lower, after, pair = [], [], []
    for l in range(nlev + 1):
        bi = i >> l
        bj = j >> l
        lower.append(((bi == bj) & (j <= i)).astype(BF16))
        after.append(((bi == bj) & (i > j)).astype(BF16))
        pair.append((bi == bj + 1) & ((bi & 1) == 1))
    diag_mask = i == j
    col_seg = [(j[0:1, :] >> nlev) == s for s in range(nb)]
    row_id = lax.broadcasted_iota(jnp.int32, (rows, HG_HEADDIM), 0)
    seg_ones = [((row_id >> nlev) == s).astype(BF16) for s in range(nb)]
    rows_of = lambda t, h: t[h * HG_HEADDIM:(h + 1) * HG_HEADDIM]

    def problem(p):
        x = _load_rows(x_ref, p, nb, x_dense)
        states = [[s_ref[p * nb + s, h] for s in range(nb)] for h in range(HG_HEADS)]
        q = _silu(x[:, 0:HG_WIDTH])
        f = lb + (1.0 - lb) * _sigmoid(x[:, HG_WIDTH:2 * HG_WIDTH])
        logf = jnp.log(f)
        k = 1.0 - f
        v = x[:, 2 * HG_WIDTH:3 * HG_WIDTH].astype(BF16)
        gate = _sigmoid(x[:, 3 * HG_WIDTH:4 * HG_WIDTH])
        k_t = k.T
        logf_t = logf.T
        logf_parts = _split(logf, 2)
        logf_t_parts = _split(logf_t, 2)
        msum = lambda m: (jnp.dot(m, logf_parts[0], preferred_element_type=F32)
                          + jnp.dot(m, logf_parts[1], preferred_element_type=F32))
        msum_t = lambda m: (jnp.dot(logf_t_parts[0], m, preferred_element_type=F32)
                            + jnp.dot(logf_t_parts[1], m, preferred_element_type=F32))

        att = [jnp.where(diag_mask, jnp.sum(hd(q, h) * hd(k, h), axis=-1, keepdims=True), 0.0)
               for h in range(HG_HEADS)]
        for l in range(nlev):
            if l == 0:
                qe, ke_t = (q * f).astype(BF16), k_t.astype(BF16)
            else:
                qe = (q * jnp.exp(msum(lower[l]))).astype(BF16)
                ke_t = (k_t * jnp.exp(msum_t(after[l]))).astype(BF16)
            for h in range(HG_HEADS):
                att[h] = att[h] + jnp.where(pair[l], _bdot(hd(qe, h), rows_of(ke_t, h)), 0.0)

        qin = (q * jnp.exp(msum(lower[nlev]))).astype(BF16)
        kend_t = (k_t * jnp.exp(msum_t(after[nlev]))).astype(BF16)
        outs, new_states = [], []
        for h in range(HG_HEADS):
            vh = hd(v, h)
            o = _bdot(att[h], vh) + cat([_bdot(seg(hd(qin, h), s), states[h][s]) for s in range(nb)])
            ns = []
            for s in range(nb):
                tot = (jnp.dot(rows_of(logf_t_parts[0], h), seg_ones[s], preferred_element_type=F32)
                       + jnp.dot(rows_of(logf_t_parts[1], h), seg_ones[s], preferred_element_type=F32))
                kend_s = rows_of(kend_t, h) if nb == 1 else jnp.where(col_seg[s], rows_of(kend_t, h), 0)
                ns.append(jnp.exp(tot) * states[h][s] + _bdot(kend_s, vh))
            new_states.append(ns)
            outs.append(_rms(o, nw_ref[...]) * hd(gate, h))
        _store_rows(y_ref, p, jnp.concatenate(outs, axis=1), nb, lseg)
        for h in range(HG_HEADS):
            for s in range(nb):
                s_ref[p * nb + s, h] = new_states[h][s]

    _for_each_problem(sub, nb, problem)


def _hgrn(x, s0, lb, norm_w, bsz, seqlen, lseg, nb, sub):
    w_in = 4 * HG_WIDTH
    ins, outs, st = _mixer_specs(lseg, sub, (w_in,), (HG_WIDTH,), s0.shape[1:])
    y, s = pl.pallas_call(
        functools.partial(_hg_kernel, lseg=lseg, nb=nb, sub=sub),
        grid=(bsz // sub, seqlen // lseg),
        in_specs=ins + [st, _const_in((1, HG_WIDTH)), _const_in((1, HG_HEADDIM))],
        out_specs=(outs[0], st),
        out_shape=(jax.ShapeDtypeStruct((seqlen, bsz, HG_WIDTH), F32), jax.ShapeDtypeStruct(s0.shape, F32)),
        scratch_shapes=[pltpu.VMEM((nb * lseg, w_in), F32)],
        compiler_params=_params(("arbitrary", "arbitrary")), name="hgrn_mixer",
    )(x.reshape(seqlen, bsz, w_in), s0, lb, norm_w)
    return y.reshape(seqlen * bsz, HG_WIDTH), s


def _rw_kernel(x_ref, s0_ref, w0_ref, wup_ref, a0_ref, aup_ref, gup_ref, kk_ref, ka_ref, rk_ref, lnw_ref, lnb_ref,
               y_ref, s_ref, x_dense, *, lseg, nb, sub):
    _init_states_transposed(s_ref, s0_ref, RW_HEADS, sub)

    R = lseg * nb
    HD, PW, NH, W = RW_HEADDIM, 2 * RW_HEADDIM, RW_HEADS, RW_WIDTH
    assert R == PW
    nlev = int(math.log2(lseg))
    i, j = _iotas(R)
    same = (i >> nlev) == (j >> nlev)
    incl = same & (j <= i)
    strict = same & (j < i)
    m_incl = incl.astype(BF16)
    m_same = same.astype(BF16)
    eye = (i == j).astype(F32)
    pair = [((i >> l) == (j >> l) + 1) & (((i >> l) & 1) == 1) for l in range(nlev)]
    m_head = ((i >> 6) == (j >> 6)).astype(BF16)
    col_seg = [(j[0:1, :] >> nlev) == s for s in range(nb)]
    seg_cols = lambda t, s: t if nb == 1 else jnp.where(col_seg[s], t, 0)
    first_col = jnp.concatenate([(i == s * lseg).astype(BF16) for s in range(nb)], axis=1)
    hd = lambda t, h: t[:, h * HD:(h + 1) * HD]
    rows_of = lambda t, h: t[h * HD:(h + 1) * HD]
    seg = lambda t, s: t[s * lseg:(s + 1) * lseg]
    cat = lambda parts: parts[0] if nb == 1 else jnp.concatenate(parts, axis=0)

    def head_sum(t):
        return jnp.concatenate([_dot_mask(t[:, q * PW:(q + 1) * PW], m_head, 2) for q in range(W // PW)], axis=1)

    def problem(p):
        x = _load_rows(x_ref, p, nb, x_dense)
        states = [[s_ref[p * nb + s, h] for s in range(nb)] for h in range(NH)]
        r, k, v = x[:, 0:W], x[:, W:2 * W], x[:, 2 * W:3 * W]
        wd, ad, gd = x[:, 3 * W:3 * W + 64], x[:, 3 * W + 64:3 * W + 128], x[:, 3 * W + 128:3 * W + 256]
        w = -jax.nn.softplus(-(w0_ref[...] + _bdot(jnp.tanh(wd), wup_ref[...]))) - 0.5
        logw = -jnp.exp(w)
        ag = _sigmoid(a0_ref[...] + _bdot(ad, aup_ref[...]))
        g = _bdot(_sigmoid(gd), gup_ref[...])
        logw_parts = _split(logw, 2)
        rsum = lambda m: (jnp.dot(m, logw_parts[0], preferred_element_type=F32)
                          + jnp.dot(m, logw_parts[1], preferred_element_type=F32))
        b_in = rsum(m_incl)
        b_tot = rsum(m_same)
        e_in = jnp.exp(b_in)
        e_ex = jnp.exp(b_in - logw)

        kk = k * kk_ref[...]
        kk = kk * lax.rsqrt(jnp.maximum(head_sum(kk * kk), 1e-24))
        k2 = k * (1.0 + (ag - 1.0) * ka_ref[...])
        kb = kk * ag
        a_t = (-kk * e_ex).astype(BF16)
        r_t = (r * e_in).astype(BF16)
        vb = v.astype(BF16)
        bonus = head_sum(r * k2 * rk_ref[...]) * v

        b_in_t = b_in.T
        b_tot_t = b_tot.T
        e_neg_t = jnp.exp(-b_in_t)
        e_end_t = jnp.exp(b_tot_t - b_in_t)
        kb_t = kb.T
        k2_t = k2.T
        b_n = (kb_t * e_neg_t).astype(BF16)
        k_n = (k2_t * e_neg_t).astype(BF16)
        b_e = (kb_t * e_end_t).astype(BF16)
        k_e = (k2_t * e_end_t).astype(BF16)
        tot_parts = _split(b_tot_t, 2)

        keys = [jnp.concatenate([rows_of(b_n, h), rows_of(k_n, h)], axis=1) for h in range(NH)]
        am = [_bdot(hd(a_t, h), keys[h]) for h in range(NH)]
        rm = [_bdot(hd(r_t, h), keys[h]) for h in range(NH)]
        a_ab = [jnp.where(strict, am[h][:, :R], 0.0) for h in range(NH)]
        a_ak = [jnp.where(strict, am[h][:, R:], 0.0) for h in range(NH)]
        a_rb = [jnp.where(incl, rm[h][:, :R], 0.0) for h in range(NH)]
        a_rk = [jnp.where(incl, rm[h][:, R:], 0.0) for h in range(NH)]
        zmat = [cat([_bdot(seg(hd(a_t, h), s), states[h][s]) for s in range(nb)]) + _bdot(a_ak[h], hd(vb, h))
                for h in range(NH)]
        y0 = [cat([_bdot(seg(hd(r_t, h), s), states[h][s]) for s in range(nb)]) + _bdot(a_rk[h], hd(vb, h))
              for h in range(NH)]
        inv = [eye + jnp.where(pair[0], a_ab[h], 0.0) for h in range(NH)]
        for l in range(1, nlev):
            tmp = [_bdot(jnp.where(pair[l], a_ab[h], 0.0), inv[h]) for h in range(NH)]
            inv = [inv[h] + _bdot(inv[h], tmp[h]) for h in range(NH)]
        sa = [_bdot(inv[h], zmat[h]) for h in range(NH)]
        ys = [y0[h] + _bdot(a_rb[h], sa[h]) for h in range(NH)]
        e_tot = [jnp.exp(jnp.dot(rows_of(tot_parts[0], h), first_col, preferred_element_type=F32)
                         + jnp.dot(rows_of(tot_parts[1], h), first_col, preferred_element_type=F32))
                 for h in range(NH)]
        new_states = [[states[h][s] * e_tot[h][:, s * PW:s * PW + HD]
                       + _bdot(seg_cols(rows_of(b_e, h), s), sa[h]) + _bdot(seg_cols(rows_of(k_e, h), s), hd(vb, h))
                       for s in range(nb)] for h in range(NH)]
        y = jnp.concatenate(ys, axis=1)
        mu = head_sum(y) * (1.0 / RW_HEADDIM)
        yc = y - mu
        var = head_sum(yc * yc) * (1.0 / RW_HEADDIM)
        y = yc * lax.rsqrt(var + RW_LN_EPS) * lnw_ref[...] + lnb_ref[...] + bonus
        _store_rows(y_ref, p, y * g, nb, lseg)
        for h in range(NH):
            for s in range(nb):
                s_ref[p * nb + s, h] = new_states[h][s]

    _for_each_problem(sub, nb, problem)
    _finish_states_transposed(s_ref, RW_HEADS, sub)


def _rwkv(x, s0, p, bsz, seqlen, lseg, nb, sub):
    ins, outs, st = _mixer_specs(lseg, sub, (RW_PROJ,), (RW_WIDTH,), s0.shape[1:])
    vec = _const_in((1, RW_WIDTH))
    y, s = pl.pallas_call(
        functools.partial(_rw_kernel, lseg=lseg, nb=nb, sub=sub),
        grid=(bsz // sub, seqlen // lseg),
        in_specs=ins + [st, vec, _const_in((64, RW_WIDTH)), vec, _const_in((64, RW_WIDTH)), _const_in((128, RW_WIDTH)),
                        vec, vec, vec, vec, vec],
        out_specs=(outs[0], st),
        out_shape=(jax.ShapeDtypeStruct((seqlen, bsz, RW_WIDTH), F32), jax.ShapeDtypeStruct(s0.shape, F32)),
        scratch_shapes=[pltpu.VMEM((nb * lseg, RW_PROJ), F32)],
        compiler_params=_params(("arbitrary", "arbitrary")), name="rwkv_mixer",
    )(x.reshape(seqlen, bsz, RW_PROJ), s0, p['w0'], p['w_up'], p['a0'], p['a_up'], p['g_up'],
      p['k_k'], p['k_a'], p['r_k'], p['ln_w'], p['ln_b'])
    return y.reshape(seqlen * bsz, RW_WIDTH), s


def _merge_kernel(x_ref, yssd_ref, ys5_ref, yhg_ref, yrw_ref, nw_ref, wm_ref, bm_ref,
                  wssd_ref, ws5_ref, whg_ref, wrw_ref, wout_ref, o_ref):
    x = x_ref[...]
    xn = _rms(x, nw_ref[...]).astype(BF16)
    merged = None
    for b, (y_ref, w_ref) in enumerate(((yssd_ref, wssd_ref), (ys5_ref, ws5_ref), (yhg_ref, whg_ref), (yrw_ref, wrw_ref))):
        cs = slice(b * D_MODEL, (b + 1) * D_MODEL)
        gate = _sigmoid(jnp.dot(xn, wm_ref[:, cs], preferred_element_type=F32) + bm_ref[:, cs])
        t = gate * jnp.dot(y_ref[...].astype(BF16), w_ref[...], preferred_element_type=F32)
        merged = t if merged is None else merged + t
    o_ref[...] = x + jnp.dot(merged.astype(BF16), wout_ref[...], preferred_element_type=F32)


def _merge(x, y_ssd, y_s5, y_hg, y_rw, tb, norm_w, w_merge, b_merge, w_ssd, w_s5, w_hg, w_rw, w_out):
    rows = x.shape[0]
    row = lambda w: pl.BlockSpec((tb, w), lambda i: (i, 0))
    return pl.pallas_call(
        _merge_kernel, grid=(rows // tb,),
        in_specs=[row(D_MODEL), row(1024), row(S5_WIDTH), row(HG_WIDTH), row(RW_WIDTH),
                  _const_in((1, D_MODEL)), _const_in((D_MODEL, 4 * D_MODEL)), _const_in((1, 4 * D_MODEL)),
                  _const_in((1024, D_MODEL)), _const_in((S5_WIDTH, D_MODEL)), _const_in((HG_WIDTH, D_MODEL)),
                  _const_in((RW_WIDTH, D_MODEL)), _const_in((D_MODEL, D_MODEL))],
        out_specs=row(D_MODEL), out_shape=jax.ShapeDtypeStruct((rows, D_MODEL), F32),
        compiler_params=_params(("arbitrary",)), name="merge",
    )(x, y_ssd, y_s5, y_hg, y_rw, norm_w, w_merge, b_merge, w_ssd, w_s5, w_hg, w_rw, w_out)


def _ffn_kernel(x_ref, nw_ref, wup_ref, cw_ref, cb_ref, wdn_ref, conv0_ref, fnw_ref, o_ref, convst_ref, full_ref,
                *, bsz, tb, final_norm):
    lookback = (FFN_CONV - 1) * bsz

    @pl.when(pl.program_id(0) == 0)
    def _():
        full_ref[0:lookback, :] = conv0_ref[...]

    x = x_ref[...]
    xn = _rms(x, nw_ref[...]).astype(BF16)
    full_ref[lookback:lookback + tb, :] = jnp.dot(xn, wup_ref[...], preferred_element_type=F32)
    acc = cb_ref[...] + full_ref[0:tb, :] * cw_ref[0:1, :]
    for j in range(1, FFN_CONV):
        acc = acc + full_ref[j * bsz:j * bsz + tb, :] * cw_ref[j:j + 1, :]
    _shift_rows_down(full_ref, tb, lookback)
    convst_ref[...] = full_ref[0:lookback, :]
    hidden = jax.nn.gelu(acc[:, 0:D_FF]) * acc[:, D_FF:2 * D_FF]
    out = x + jnp.dot(hidden.astype(BF16), wdn_ref[...], preferred_element_type=F32)
    if final_norm:
        out = _rms(out, fnw_ref[...])
    o_ref[...] = out


def _ffn(x, bsz, tb, norm_w, w_up, conv_w, conv_b, w_down, conv0, final_w, final_norm):
    rows = x.shape[0]
    lookback = (FFN_CONV - 1) * bsz
    assert rows % tb == 0 and tb % bsz == 0
    row = pl.BlockSpec((tb, D_MODEL), lambda i: (i, 0))
    return pl.pallas_call(
        functools.partial(_ffn_kernel, bsz=bsz, tb=tb, final_norm=final_norm),
        grid=(rows // tb,),
        in_specs=[row, _const_in((1, D_MODEL)), _const_in((D_MODEL, 2 * D_FF)), _const_in((FFN_CONV, 2 * D_FF)),
                  _const_in((1, 2 * D_FF)), _const_in((D_FF, D_MODEL)), _const_in((lookback, 2 * D_FF)),
                  _const_in((1, D_MODEL))],
        out_specs=(row, _const_spec((lookback, 2 * D_FF))),
        out_shape=(jax.ShapeDtypeStruct((rows, D_MODEL), F32), jax.ShapeDtypeStruct((lookback, 2 * D_FF), F32)),
        scratch_shapes=[pltpu.VMEM((lookback + tb, 2 * D_FF), F32)],
        compiler_params=_params(("arbitrary",)), name="conv_ffn",
    )(x, norm_w, w_up, conv_w, conv_b, w_down, conv0, final_w)


def _pad_lanes(v, width=DT_PAD):
    return jnp.pad(v, (0, width - v.shape[0])).reshape(1, width)


def _layer_params(l, P, lb_all):
    p = {n: a[l] for n, a in P.items()}
    row = lambda a: a.reshape(1, -1)
    ab_re, ab_im, s5_win, s5_wcre, s5_wcim = _s5_prep(p['s5_log_dt'], p['s5_a_re'], p['s5_a_im'], p['s5_b_re'],
                                                      p['s5_b_im'], p['s5_c_re'], p['s5_c_im'])
    q = dict(
        norm1_w=row(p['norm1_w']), w_in=_pack_w_in(p['w_in']),
        ssd_conv_w=p['ssd_conv_w'], ssd_conv_b=row(p['ssd_conv_b']), ssd_dt_bias=_pad_lanes(p['ssd_dt_bias']),
        ssd_a_log=_pad_lanes(p['ssd_a_log']), ssd_d=_pad_lanes(p['ssd_d']), ssd_norm_w=row(p['ssd_norm_w']),
        s5_win=s5_win, s5_wcre=s5_wcre, s5_wcim=s5_wcim,
        s5_ab_re=row(ab_re), s5_ab_im=row(ab_im), s5_d=row(p['s5_d']),
        s5_glu_w=p['s5_glu_w'].astype(BF16), s5_glu_b=row(p['s5_glu_b']),
        hg_lb=lb_all[l:l + 1], hg_norm_w=row(p['hg_norm_w']),
        rw_mu=row(p['rw_mu']),
        rw=dict(w0=row(p['rw_w0']), w_up=p['rw_w_up'].astype(BF16), a0=row(p['rw_a0']), a_up=p['rw_a_up'].astype(BF16),
                g_up=p['rw_g_up'].astype(BF16), k_k=row(p['rw_k_k']), k_a=row(p['rw_k_a']), r_k=row(p['rw_r_k']),
                ln_w=row(p['rw_ln_w']), ln_b=row(p['rw_ln_b'])),
        w_merge=p['w_merge'].astype(BF16), b_merge=row(p['b_merge']),
        w_br_ssd=p['w_br_ssd'].astype(BF16), w_br_s5=p['w_br_s5'].astype(BF16), w_br_hg=p['w_br_hg'].astype(BF16),
        w_br_rw=p['w_br_rw'].astype(BF16), w_out=p['w_out'].astype(BF16),
        norm2_w=row(p['norm2_w']), ffn_up=p['ffn_up'].astype(BF16), ffn_conv_w=p['ffn_conv_w'],
        ffn_conv_b=row(p['ffn_conv_b']), ffn_down=p['ffn_down'].astype(BF16),
    )
    return q


class _Group:
    def __init__(self, bsz, seqlen):
        self.bsz, self.seqlen = bsz, seqlen
        rows = bsz * seqlen
        self.tb = min(rows, max(256, bsz))
        self.tt = self.tb // bsz
        if seqlen >= 128:
            self.ssd, self.hg, self.rw = (128, 2, 8), (64, 2, 8), (64, 2, 8)
        else:
            self.ssd = self.hg = self.rw = (seqlen, 128 // seqlen, 128 // seqlen)


def _time_major(a):
    a = jnp.swapaxes(a, 0, 1)
    return a.reshape((a.shape[0] * a.shape[1],) + a.shape[2:])


def _batch_major(a, bsz):
    a = a.reshape((a.shape[0] // bsz, bsz) + a.shape[1:])
    return jnp.swapaxes(a, 0, 1)


def _trunk(x, states, layers, final_norm_w, grp):
    bsz, seqlen = grp.bsz, grp.seqlen
    xt = _time_major(x)
    new_states = []
    for l, q in enumerate(layers):
        s_ssd, s_conv, s_s5r, s_s5i, s_hg, s_rw, s_shift, s_fconv = states[l]
        z, xbc, dt, u, hg, rwx, conv_new, shift_new = _in_proj(
            xt, bsz, grp.tb, q['norm1_w'], q['w_in'], q['ssd_conv_w'], q['ssd_conv_b'], q['ssd_dt_bias'], q['rw_mu'],
            _time_major(s_conv), s_shift)
        y_ssd, ssd_new = _ssd(z, xbc, dt, s_ssd, q['ssd_a_log'], q['ssd_d'], q['ssd_norm_w'],
                              bsz, seqlen, *grp.ssd)
        y_s5, hr_new, hi_new = _s5(u, bsz, grp.tt, q['s5_win'], q['s5_wcre'], q['s5_wcim'], q['s5_ab_re'], q['s5_ab_im'],
                                   q['s5_d'], q['s5_glu_w'], q['s5_glu_b'],
                                   s_s5r.reshape(bsz, S5_CH), s_s5i.reshape(bsz, S5_CH))
        y_hg, hg_new = _hgrn(hg, s_hg, q['hg_lb'], q['hg_norm_w'], bsz, seqlen, *grp.hg)
        y_rw, rw_new = _rwkv(rwx, s_rw, q['rw'], bsz, seqlen, *grp.rw)
        x1 = _merge(xt, y_ssd, y_s5, y_hg, y_rw, grp.tb, q['norm1_w'], q['w_merge'], q['b_merge'],
                    q['w_br_ssd'], q['w_br_s5'], q['w_br_hg'], q['w_br_rw'], q['w_out'])
        xt, fconv_new = _ffn(x1, bsz, grp.tb, q['norm2_w'], q['ffn_up'], q['ffn_conv_w'], q['ffn_conv_b'],
                             q['ffn_down'], _time_major(s_fconv), final_norm_w.reshape(1, D_MODEL), l == DEPTH - 1)
        new_states.append((ssd_new, _batch_major(conv_new, bsz),
                           hr_new.reshape(bsz, S5_GROUPS, S5_STATE), hi_new.reshape(bsz, S5_GROUPS, S5_STATE),
                           hg_new, rw_new, shift_new, _batch_major(fconv_new, bsz)))
    stacked = tuple(jnp.stack([st[k] for st in new_states], axis=0) for k in range(8))
    return _batch_major(xt, bsz), stacked


def _zero_states(bsz):
    z = lambda *s: jnp.zeros((bsz,) + s, F32)
    return (z(SSD_HEADS, SSD_HEADDIM, SSD_D_STATE), z(SSD_CONV - 1, SSD_CONV_CH), z(S5_GROUPS, S5_STATE),
            z(S5_GROUPS, S5_STATE), z(HG_HEADS, HG_HEADDIM, HG_HEADDIM), z(RW_HEADS, RW_HEADDIM, RW_HEADDIM),
            z(RW_PROJ), z(FFN_CONV - 1, 2 * D_FF))


def kernel(x_prompt, x_sample, state_ssd, state_ssd_conv, state_s5_re, state_s5_im, state_hgrn, state_rwkv, state_rwkv_shift, state_ffn_conv, norm1_w, w_in, ssd_conv_w, ssd_conv_b, ssd_dt_bias, ssd_a_log, ssd_d, ssd_norm_w, s5_a_re, s5_a_im, s5_log_dt, s5_b_re, s5_b_im, s5_c_re, s5_c_im, s5_d, s5_glu_w, s5_glu_b, hg_lb_raw, hg_norm_w, rw_mu, rw_w0, rw_w_up, rw_a0, rw_a_up, rw_g_up, rw_k_k, rw_k_a, rw_r_k, rw_ln_w, rw_ln_b, w_br_ssd, w_br_s5, w_br_hg, w_br_rw, w_merge, b_merge, w_out, norm2_w, ffn_up, ffn_conv_w, ffn_conv_b, ffn_down, final_norm_w):
    P = dict(norm1_w=norm1_w, w_in=w_in, ssd_conv_w=ssd_conv_w, ssd_conv_b=ssd_conv_b, ssd_dt_bias=ssd_dt_bias,
             ssd_a_log=ssd_a_log, ssd_d=ssd_d, ssd_norm_w=ssd_norm_w, s5_a_re=s5_a_re, s5_a_im=s5_a_im,
             s5_log_dt=s5_log_dt, s5_b_re=s5_b_re, s5_b_im=s5_b_im, s5_c_re=s5_c_re, s5_c_im=s5_c_im, s5_d=s5_d,
             s5_glu_w=s5_glu_w, s5_glu_b=s5_glu_b, hg_norm_w=hg_norm_w, rw_mu=rw_mu, rw_w0=rw_w0, rw_w_up=rw_w_up,
             rw_a0=rw_a0, rw_a_up=rw_a_up, rw_g_up=rw_g_up,
             rw_k_k=rw_k_k.reshape(DEPTH, RW_WIDTH), rw_k_a=rw_k_a.reshape(DEPTH, RW_WIDTH),
             rw_r_k=rw_r_k.reshape(DEPTH, RW_WIDTH), rw_ln_w=rw_ln_w.reshape(DEPTH, RW_WIDTH),
             rw_ln_b=rw_ln_b.reshape(DEPTH, RW_WIDTH),
             w_br_ssd=w_br_ssd, w_br_s5=w_br_s5, w_br_hg=w_br_hg, w_br_rw=w_br_rw, w_merge=w_merge, b_merge=b_merge,
             w_out=w_out, norm2_w=norm2_w, ffn_up=ffn_up, ffn_conv_w=ffn_conv_w, ffn_conv_b=ffn_conv_b,
             ffn_down=ffn_down)
    lb_all = _lower_bounds(hg_lb_raw)
    layers = [_layer_params(l, P, lb_all) for l in range(DEPTH)]
    sample_states = (state_ssd, state_ssd_conv, state_s5_re, state_s5_im, state_hgrn, state_rwkv,
                     state_rwkv_shift, state_ffn_conv)
    sample_init = [tuple(s[l] for s in sample_states) for l in range(DEPTH)]
    prompt_init = [_zero_states(x_prompt.shape[0])] * DEPTH
    y_prompt, p_states = _trunk(x_prompt, prompt_init, layers, final_norm_w, _Group(*x_prompt.shape[:2]))
    y_sample, s_states = _trunk(x_sample, sample_init, layers, final_norm_w, _Group(*x_sample.shape[:2]))
    return (y_prompt, y_sample) + p_states + s_states
```

```python
import functools
import math

import jax
import jax.numpy as jnp
from jax import lax
from jax.experimental import pallas as pl
from jax.experimental.pallas import tpu as pltpu

F32 = jnp.float32
BF16 = jnp.bfloat16

D_MODEL = 1024
DEPTH = 4
SSD_HEADS = 16
SSD_HEADDIM = 64
SSD_D_STATE = 64
SSD_GROUPS = 4
SSD_CONV = 4
SSD_CONV_CH = 1536
S5_WIDTH = 512
S5_GROUPS = 32
S5_GROUP = 16
S5_STATE = 64
S5_CH = S5_GROUPS * S5_STATE
HG_WIDTH = 512
HG_HEADS = 4
HG_HEADDIM = 128
RW_WIDTH = 512
RW_HEADS = 8
RW_HEADDIM = 64
RW_PROJ = 1792
RW_LN_EPS = 64e-5
D_FF = 2816
FFN_CONV = 3
EPS = 1e-6
IN_SIZES = (1024, 1536, 16, 512, 512, 512, 512, 512, 1792)
DT_PAD = 128

VMEM_LIMIT = 56 * 1024 * 1024


def _bdot(a, b):
    return jnp.dot(a.astype(BF16), b.astype(BF16), preferred_element_type=F32)


def _split(x, n):
    parts = []
    r = x
    for _ in range(n):
        p = r.astype(BF16)
        parts.append(p)
        r = r - p.astype(F32)
    return parts


def _mask_dot(m, x, n=3):
    out = None
    for p in _split(x, n):
        t = jnp.dot(m, p, preferred_element_type=F32)
        out = t if out is None else out + t
    return out


def _dot_mask(x, m, n=3):
    out = None
    for p in _split(x, n):
        t = jnp.dot(p, m, preferred_element_type=F32)
        out = t if out is None else out + t
    return out


def _rms(x, w):
    return x * lax.rsqrt(jnp.mean(x * x, axis=-1, keepdims=True) + EPS) * w


def _sigmoid(x):
    return jax.nn.sigmoid(x)


def _silu(x):
    return x * jax.nn.sigmoid(x)


def _load_rows(ref, p, nb, dense_ref):
    lseg = ref.shape[0]
    for s in range(nb):
        dense_ref[s * lseg:(s + 1) * lseg, :] = ref[:, p * nb + s, :]
    return dense_ref[...]


def _store_rows(ref, p, y, nb, lseg):
    for s in range(nb):
        ref[:, p * nb + s, :] = y[s * lseg:(s + 1) * lseg, :]


def _init_states_transposed(s_ref, s0_ref, nheads, sub):
    @pl.when(pl.program_id(1) == 0)
    def _():
        def body(b, carry):
            for h in range(nheads):
                s_ref[b, h] = s0_ref[b, h].T
            return carry
        lax.fori_loop(0, sub, body, 0)


def _finish_states_transposed(s_ref, nheads, sub):
    @pl.when(pl.program_id(1) == pl.num_programs(1) - 1)
    def _():
        def body(b, carry):
            for h in range(nheads):
                s_ref[b, h] = s_ref[b, h].T
            return carry
        lax.fori_loop(0, sub, body, 0)


def _for_each_problem(sub, nb, problem):
    if nb == sub:
        problem(0)
    else:
        def body(p, carry):
            problem(p)
            return carry
        lax.fori_loop(0, sub // nb, body, 0)


def _shift_rows_down(ref, dist, n):
    for off in range(0, n, dist):
        m = min(dist, n - off)
        ref[off:off + m, :] = ref[off + dist:off + dist + m, :]


def _iotas(rows):
    i = lax.broadcasted_iota(jnp.int32, (rows, rows), 0)
    j = lax.broadcasted_iota(jnp.int32, (rows, rows), 1)
    return i, j


def _const_spec(shape):
    nd = len(shape)
    return pl.BlockSpec(shape, lambda *_: (0,) * nd)


def _const_in(shape):
    nd = len(shape)
    return pl.BlockSpec(shape, lambda *_: (0,) * nd, pipeline_mode=pl.Buffered(1))


def _params(sem):
    return pltpu.CompilerParams(dimension_semantics=sem, vmem_limit_bytes=VMEM_LIMIT)


def _lb_kernel(raw_ref, o_ref):
    raw = raw_ref[...]
    m = jnp.max(raw, axis=0, keepdims=True)
    e = jnp.exp(raw - m)
    sm = e / jnp.sum(e, axis=0, keepdims=True)
    acc = jnp.zeros_like(sm[0:1])
    rows = []
    for l in range(DEPTH):
        acc = acc + sm[l:l + 1]
        rows.append(acc)
    first = rows[0]
    o_ref[...] = jnp.concatenate([r - first for r in rows], axis=0)


def _lower_bounds(hg_lb_raw):
    return pl.pallas_call(_lb_kernel, out_shape=jax.ShapeDtypeStruct((DEPTH, HG_WIDTH), F32),
                          name="hg_lower_bounds")(hg_lb_raw)


S5_PACK = 8
S5_SLABS = S5_GROUPS // S5_PACK
S5_SLAB_IN = S5_PACK * S5_GROUP
S5_SLAB_CH = S5_PACK * S5_STATE


def _s5_prep_kernel(ldt_ref, are_ref, aim_ref, bre_ref, bim_ref, cre_ref, cim_ref,
                    abre_ref, abim_ref, win_ref, wcre_ref, wcim_ref):
    dt = jnp.exp(ldt_ref[...])
    a_re = are_ref[...]
    a_im = aim_ref[...]
    mag = jnp.exp(dt * a_re)
    ab_re = mag * jnp.cos(dt * a_im)
    ab_im = mag * jnp.sin(dt * a_im)
    den = a_re * a_re + a_im * a_im
    q_re = ((ab_re - 1.0) * a_re + ab_im * a_im) / den
    q_im = (ab_im * a_re - (ab_re - 1.0) * a_im) / den
    abre_ref[...] = ab_re
    abim_ref[...] = ab_im
    win_ref[...] = jnp.zeros(win_ref.shape, win_ref.dtype)
    wcre_ref[...] = jnp.zeros(wcre_ref.shape, wcre_ref.dtype)
    wcim_ref[...] = jnp.zeros(wcim_ref.shape, wcim_ref.dtype)
    for g in range(S5_GROUPS):
        slab, r = divmod(g, S5_PACK)
        rows_in = slice(r * S5_GROUP, (r + 1) * S5_GROUP)
        ch = slice(r * S5_STATE, (r + 1) * S5_STATE)
        ch_im = slice(S5_SLAB_CH + r * S5_STATE, S5_SLAB_CH + (r + 1) * S5_STATE)
        b_re = bre_ref[g * S5_GROUP:(g + 1) * S5_GROUP, :]
        b_im = bim_ref[g * S5_GROUP:(g + 1) * S5_GROUP, :]
        win_ref[slab, rows_in, ch] = (q_re[g:g + 1] * b_re - q_im[g:g + 1] * b_im).astype(win_ref.dtype)
        win_ref[slab, rows_in, ch_im] = (q_re[g:g + 1] * b_im + q_im[g:g + 1] * b_re).astype(win_ref.dtype)
        wcre_ref[slab, ch, rows_in] = cre_ref[g * S5_STATE:(g + 1) * S5_STATE, :].astype(wcre_ref.dtype)
        wcim_ref[slab, ch, rows_in] = cim_ref[g * S5_STATE:(g + 1) * S5_STATE, :].astype(wcim_ref.dtype)


def _s5_prep(log_dt, a_re, a_im, b_re, b_im, c_re, c_im):
    bt_re = jnp.swapaxes(b_re, 1, 2).reshape(S5_WIDTH, S5_STATE)
    bt_im = jnp.swapaxes(b_im, 1, 2).reshape(S5_WIDTH, S5_STATE)
    ct_re = jnp.swapaxes(c_re, 1, 2).reshape(S5_CH, S5_GROUP)
    ct_im = jnp.swapaxes(c_im, 1, 2).reshape(S5_CH, S5_GROUP)
    gn = jax.ShapeDtypeStruct((S5_GROUPS, S5_STATE), F32)
    return pl.pallas_call(
        _s5_prep_kernel,
        out_shape=(gn, gn, jax.ShapeDtypeStruct((S5_SLABS, S5_SLAB_IN, 2 * S5_SLAB_CH), BF16),
                   jax.ShapeDtypeStruct((S5_SLABS, S5_SLAB_CH, S5_SLAB_IN), BF16),
                   jax.ShapeDtypeStruct((S5_SLABS, S5_SLAB_CH, S5_SLAB_IN), BF16)),
        name="s5_discretise")(log_dt.reshape(S5_GROUPS, 1), a_re, a_im, bt_re, bt_im, ct_re, ct_im)


SEG_Z = (0, 1024)
SEG_XBC = (1024, 1536)
SEG_U = (2560, 512)
SEG_HG = (3072, 2048)
SEG_RW = (5120, 1792)
SEG_DT = (6912, DT_PAD)
IN_PACKED = 7040


def _pack_w_in(w_in):
    off = [0]
    for s in IN_SIZES:
        off.append(off[-1] + s)
    z, xbc, dt, u, q, f, i, g, rw = (w_in[:, off[k]:off[k + 1]] for k in range(9))
    dt = jnp.pad(dt, ((0, 0), (0, DT_PAD - dt.shape[1])))
    return jnp.concatenate([z, xbc, u, q, f, i, g, rw, dt], axis=1).astype(BF16)


def _in_kernel(x_ref, nw_ref, w_ref, cw_ref, cb_ref, dtb_ref, mu_ref, conv0_ref, sh0_ref,
               z_ref, xbc_ref, dt_ref, u_ref, hg_ref, rw_ref, convst_ref, shst_ref,
               full_ref, rwfull_ref, *, bsz, tb):
    step = pl.program_id(0)
    lookback = (SSD_CONV - 1) * bsz

    @pl.when(step == 0)
    def _():
        full_ref[0:lookback, :] = conv0_ref[...]
        rwfull_ref[0:bsz, :] = sh0_ref[...]

    xn = _rms(x_ref[...], nw_ref[...]).astype(BF16)

    def proj(seg):
        return jnp.dot(xn, w_ref[:, seg[0]:seg[0] + seg[1]], preferred_element_type=F32)

    z_ref[...] = proj(SEG_Z)
    u_ref[...] = proj(SEG_U)
    hg_ref[...] = proj(SEG_HG)
    dt_ref[...] = jax.nn.softplus(proj(SEG_DT) + dtb_ref[...])

    full_ref[lookback:lookback + tb, :] = proj(SEG_XBC)
    acc = cb_ref[...] + full_ref[0:tb, :] * cw_ref[0:1, :]
    for j in range(1, SSD_CONV):
        acc = acc + full_ref[j * bsz:j * bsz + tb, :] * cw_ref[j:j + 1, :]
    xbc_ref[...] = _silu(acc)
    _shift_rows_down(full_ref, tb, lookback)
    convst_ref[...] = full_ref[0:lookback, :]

    rwfull_ref[bsz:bsz + tb, :] = proj(SEG_RW)
    cur = rwfull_ref[bsz:bsz + tb, :]
    prev = rwfull_ref[0:tb, :]
    rw_ref[...] = cur + (prev - cur) * mu_ref[...]
    last = rwfull_ref[tb:tb + bsz, :]
    rwfull_ref[0:bsz, :] = last
    shst_ref[...] = last


def _in_proj(x, bsz, tb, norm_w, w_packed, conv_w, conv_b, dt_bias, mu, conv0, shift0):
    rows = x.shape[0]
    lookback = (SSD_CONV - 1) * bsz
    assert rows % tb == 0 and tb % bsz == 0
    row = lambda w: pl.BlockSpec((tb, w), lambda i: (i, 0))
    widths = (1024, SSD_CONV_CH, DT_PAD, S5_WIDTH, 4 * HG_WIDTH, RW_PROJ)
    out_shape = tuple(jax.ShapeDtypeStruct((rows, w), F32) for w in widths) + (
        jax.ShapeDtypeStruct((lookback, SSD_CONV_CH), F32), jax.ShapeDtypeStruct((bsz, RW_PROJ), F32))
    out_specs = tuple(row(w) for w in widths) + (_const_spec((lookback, SSD_CONV_CH)), _const_spec((bsz, RW_PROJ)))
    in_specs = [row(D_MODEL), _const_in((1, D_MODEL)), _const_in((D_MODEL, IN_PACKED)),
                _const_in((SSD_CONV, SSD_CONV_CH)), _const_in((1, SSD_CONV_CH)), _const_in((1, DT_PAD)),
                _const_in((1, RW_PROJ)), _const_in((lookback, SSD_CONV_CH)), _const_in((bsz, RW_PROJ))]
    return pl.pallas_call(
        functools.partial(_in_kernel, bsz=bsz, tb=tb),
        grid=(rows // tb,), in_specs=in_specs, out_specs=out_specs, out_shape=out_shape,
        scratch_shapes=[pltpu.VMEM((lookback + tb, SSD_CONV_CH), F32), pltpu.VMEM((bsz + tb, RW_PROJ), F32)],
        compiler_params=_params(("arbitrary",)), name="in_proj",
    )(x, norm_w, w_packed, conv_w, conv_b, dt_bias, mu, conv0, shift0)


def _ssd_kernel(z_ref, xbc_ref, dt_ref, s0_ref, alog_ref, d_ref, nw_ref, y_ref, s_ref,
                z_dense, xbc_dense, dt_dense, *, lseg, nb, sub):
    _init_states_transposed(s_ref, s0_ref, SSD_HEADS, sub)

    rows = lseg * nb
    sh = int(math.log2(lseg))
    n_x = SSD_HEADS * SSD_HEADDIM
    gw = SSD_GROUPS * SSD_D_STATE
    hpg = SSD_HEADS // SSD_GROUPS
    neg_a = -jnp.exp(alog_ref[...])
    d_skip = d_ref[...]
    i, j = _iotas(rows)
    same = (i >> sh) == (j >> sh)
    tril = same & (j <= i)
    m_tril = tril.astype(BF16)
    m_triu = (same & (i <= j)).astype(BF16)
    m_same = same.astype(BF16)
    seg = lambda t, s: t[s * lseg:(s + 1) * lseg]
    cat = lambda parts: parts[0] if nb == 1 else jnp.concatenate(parts, axis=0)

    col_seg = [(j[0:1, :] >> sh) == s for s in range(nb)]

    def problem(p):
        z = _load_rows(z_ref, p, nb, z_dense)
        xbc = _load_rows(xbc_ref, p, nb, xbc_dense)
        dt = _load_rows(dt_ref, p, nb, dt_dense)
        states = [[s_ref[p * nb + s, h] for s in range(nb)] for h in range(SSD_HEADS)]
        a = dt * neg_a
        cum = _mask_dot(m_tril, a, 2)
        cum_t = _dot_mask(a.T, m_triu, 2)
        tot = _mask_dot(m_same, a, 2)
        dec_end = jnp.exp(tot - cum)
        dec_in = jnp.exp(cum)
        dec_tot = jnp.exp(tot)
        bm_t = xbc[:, n_x:n_x + gw].T.astype(BF16)
        bts = [bm_t[g * SSD_D_STATE:(g + 1) * SSD_D_STATE] for g in range(SSD_GROUPS)]
        if nb > 1:
            zero = jnp.zeros_like(bts[0])
            bts_seg = [[jnp.where(col_seg[s], bts[g], zero) for s in range(nb)] for g in range(SSD_GROUPS)]
        else:
            bts_seg = [[bts[g]] for g in range(SSD_GROUPS)]
        cms = [xbc[:, n_x + gw + g * SSD_D_STATE:n_x + gw + (g + 1) * SSD_D_STATE].astype(BF16)
               for g in range(SSD_GROUPS)]
        gmats = [_bdot(cms[g], bts[g]) for g in range(SSD_GROUPS)]
        ys, new_states = [], []
        for h in range(SSD_HEADS):
            g = h // hpg
            cm = cms[g]
            xh = xbc[:, h * SSD_HEADDIM:(h + 1) * SSD_HEADDIM]
            xdt = xh * dt[:, h:h + 1]
            diff = cum[:, h:h + 1] - cum_t[h:h + 1, :]
            lmat = jnp.where(tril, jnp.exp(jnp.minimum(diff, 0.0)), 0.0)
            y = _bdot(gmats[g] * lmat, xdt)
            xdec = (xdt * dec_end[:, h:h + 1]).astype(BF16)
            y_off = cat([_bdot(seg(cm, s), states[h][s]) for s in range(nb)]) * dec_in[:, h:h + 1]
            new_states.append([states[h][s] * dec_tot[s * lseg:s * lseg + 1, h:h + 1]
                               + _bdot(bts_seg[g][s], xdec) for s in range(nb)])
            ys.append(y + y_off + xh * d_skip[:, h:h + 1])
        yall = jnp.concatenate(ys, axis=1)
        yall = _rms(yall * _silu(z), nw_ref[...])
        _store_rows(y_ref, p, yall, nb, lseg)
        for h in range(SSD_HEADS):
            for s in range(nb):
                s_ref[p * nb + s, h] = new_states[h][s]

    _for_each_problem(sub, nb, problem)
    _finish_states_transposed(s_ref, SSD_HEADS, sub)


def _without_ref(fn, idx):
    def body(*refs):
        return fn(*refs[:idx], *refs[idx + 1:])
    return body


def _mixer_call(kernel_fn, name, xs, s0, consts, const_specs, stacked, layer, width_out, scratch_widths,
                bsz, seqlen, lseg, nb, sub):
    blk = lambda w: pl.BlockSpec((lseg, sub, w), lambda b, c: (c, b, 0))
    zeros = (0,) * (s0.ndim - 1)
    st_in = pl.BlockSpec((sub,) + s0.shape[1:], lambda b, c: (b,) + zeros)
    st_out = pl.BlockSpec((None, sub) + s0.shape[1:], lambda b, c: (layer, b) + zeros)
    in_specs = [blk(x.shape[1]) for x in xs] + [st_in] + list(const_specs)
    args = [x.reshape(seqlen, bsz, x.shape[1]) for x in xs] + [s0] + list(consts)
    body = functools.partial(kernel_fn, lseg=lseg, nb=nb, sub=sub)
    aliases = {}
    if stacked is not None:
        aliases = {len(args): 1}
        body = _without_ref(body, len(args))
        in_specs.append(pl.BlockSpec(memory_space=pl.ANY))
        args.append(stacked)
    y, s = pl.pallas_call(
        body, grid=(bsz // sub, seqlen // lseg), in_specs=in_specs, out_specs=(blk(width_out), st_out),
        out_shape=(jax.ShapeDtypeStruct((seqlen, bsz, width_out), F32),
                   jax.ShapeDtypeStruct((DEPTH,) + s0.shape, F32)),
        scratch_shapes=[pltpu.VMEM((nb * lseg, w), F32) for w in scratch_widths],
        input_output_aliases=aliases, compiler_params=_params(("arbitrary", "arbitrary")), name=name,
    )(*args)
    return y.reshape(seqlen * bsz, width_out), s


def _ssd(z, xbc, dt, s0, a_log, d_skip, norm_w, stacked, layer, bsz, seqlen, lseg, nb, sub):
    n_x = SSD_HEADS * SSD_HEADDIM
    return _mixer_call(_ssd_kernel, "ssd_mixer", (z, xbc, dt), s0, (a_log, d_skip, norm_w),
                       (_const_in((1, DT_PAD)), _const_in((1, DT_PAD)), _const_in((1, n_x))), stacked, layer,
                       n_x, (n_x, SSD_CONV_CH, DT_PAD), bsz, seqlen, lseg, nb, sub)


def _s5_kernel(u_ref, win_ref, wcre_ref, wcim_ref, abre_ref, abim_ref, d_ref, gw_ref, gb_ref, hr0_ref, hi0_ref,
               y_ref, hr_ref, hi_ref, bu_ref, *, bsz, tt):
    @pl.when(pl.program_id(0) == 0)
    def _():
        hr_ref[...] = hr0_ref[...]
        hi_ref[...] = hi0_ref[...]

    u = u_ref[...]
    ub = u.astype(BF16)
    for q in range(S5_SLABS):
        t = jnp.dot(ub[:, q * S5_SLAB_IN:(q + 1) * S5_SLAB_IN], win_ref[q], preferred_element_type=F32)
        bu_ref[:, q * S5_SLAB_CH:(q + 1) * S5_SLAB_CH] = t[:, 0:S5_SLAB_CH]
        bu_ref[:, S5_CH + q * S5_SLAB_CH:S5_CH + (q + 1) * S5_SLAB_CH] = t[:, S5_SLAB_CH:2 * S5_SLAB_CH]
    ab_re = abre_ref[...]
    ab_im = abim_ref[...]

    def step(t, carry):
        r0 = pl.multiple_of(t * bsz, bsz)
        hr = hr_ref[...]
        hi = hi_ref[...]
        nr = ab_re * hr - ab_im * hi + bu_ref[pl.ds(r0, bsz), 0:S5_CH]
        ni = ab_re * hi + ab_im * hr + bu_ref[pl.ds(r0, bsz), S5_CH:2 * S5_CH]
        hr_ref[...] = nr
        hi_ref[...] = ni
        bu_ref[pl.ds(r0, bsz), 0:S5_CH] = nr
        bu_ref[pl.ds(r0, bsz), S5_CH:2 * S5_CH] = ni
        return carry

    lax.fori_loop(0, tt, step, 0)
    ys = []
    for q in range(S5_SLABS):
        h_re = bu_ref[:, q * S5_SLAB_CH:(q + 1) * S5_SLAB_CH].astype(BF16)
        h_im = bu_ref[:, S5_CH + q * S5_SLAB_CH:S5_CH + (q + 1) * S5_SLAB_CH].astype(BF16)
        ys.append(jnp.dot(h_re, wcre_ref[q], preferred_element_type=F32)
                  - jnp.dot(h_im, wcim_ref[q], preferred_element_type=F32))
    y = jnp.concatenate(ys, axis=1)
    y = jax.nn.gelu(y + d_ref[...] * u)
    y_ref[...] = y * _sigmoid(jnp.dot(y.astype(BF16), gw_ref[...], preferred_element_type=F32) + gb_ref[...])


def _s5(u, bsz, tt, w_in_bd, w_cre_bd, w_cim_bd, ab_re, ab_im, d_skip, glu_w, glu_b, hr0, hi0):
    rows = u.shape[0]
    tb = tt * bsz
    assert rows % tb == 0
    row = pl.BlockSpec((tb, S5_WIDTH), lambda i: (i, 0))
    st = _const_spec((bsz, S5_CH))
    st_in = _const_in((bsz, S5_CH))
    return pl.pallas_call(
        functools.partial(_s5_kernel, bsz=bsz, tt=tt),
        grid=(rows // tb,),
        in_specs=[row, _const_in((S5_SLABS, S5_SLAB_IN, 2 * S5_SLAB_CH)), _const_in((S5_SLABS, S5_SLAB_CH, S5_SLAB_IN)),
                  _const_in((S5_SLABS, S5_SLAB_CH, S5_SLAB_IN)), _const_in((1, S5_CH)), _const_in((1, S5_CH)),
                  _const_in((1, S5_WIDTH)), _const_in((S5_WIDTH, S5_WIDTH)), _const_in((1, S5_WIDTH)), st_in, st_in],
        out_specs=(row, st, st),
        out_shape=(jax.ShapeDtypeStruct((rows, S5_WIDTH), F32), jax.ShapeDtypeStruct((bsz, S5_CH), F32),
                   jax.ShapeDtypeStruct((bsz, S5_CH), F32)),
        scratch_shapes=[pltpu.VMEM((tb, 2 * S5_CH), F32)],
        compiler_params=_params(("arbitrary",)), name="s5_mixer",
    )(u, w_in_bd, w_cre_bd, w_cim_bd, ab_re, ab_im, d_skip, glu_w, glu_b, hr0, hi0)


def _hg_kernel(x_ref, s0_ref, lb_ref, nw_ref, y_ref, s_ref, x_dense, *, lseg, nb, sub):
    @pl.when(pl.program_id(1) == 0)
    def _():
        s_ref[...] = s0_ref[...]

    rows = lseg * nb
    nlev = int(math.log2(lseg))
    lb = lb_ref[...]
    i, j = _iotas(rows)
    hd = lambda t, h: t[:, h * HG_HEADDIM:(h + 1) * HG_HEADDIM]
    seg = lambda t, s: t[s * lseg:(s + 1) * lseg]
    cat = lambda parts: parts[0] if nb == 1 else jnp.concatenate(parts, axis=0)

    SMALL = 8
    lower, upper, pair = {}, {}, []
    for l in range(nlev + 1):
        bi = i >> l
        bj = j >> l
        if (1 << l) < SMALL or l == nlev:
            lower[l] = ((bi == bj) & (j <= i)).astype(BF16)
            upper[l] = ((bi == bj) & (j > i)).astype(BF16)
        pair.append((bi == bj + 1) & ((bi & 1) == 1))
    diag_mask = i == j
    col_seg = [(j[0:1, :] >> nlev) == s for s in range(nb)]
    rows_of = lambda t, h: t[h * HG_HEADDIM:(h + 1) * HG_HEADDIM]

    def block_row(t, size, which):
        t3 = t.reshape(rows // size, size, t.shape[1])
        return jnp.broadcast_to(t3[:, which:which + 1, :], t3.shape).reshape(t.shape)

    def problem(p):
        x = _load_rows(x_ref, p, nb, x_dense)
        states = [[s_ref[p * nb + s, h] for s in range(nb)] for h in range(HG_HEADS)]
        q = _silu(x[:, 0:HG_WIDTH])
        f = lb + (1.0 - lb) * _sigmoid(x[:, HG_WIDTH:2 * HG_WIDTH])
        logf = jnp.log(f)
        k = 1.0 - f
        v = x[:, 2 * HG_WIDTH:3 * HG_WIDTH].astype(BF16)
        gate = _sigmoid(x[:, 3 * HG_WIDTH:4 * HG_WIDTH])
        logf_parts = _split(logf, 3)
        msum = lambda m, n: sum(jnp.dot(m, part, preferred_element_type=F32) for part in logf_parts[:n])
        cum = msum(lower[nlev], 3)
        cum_ex = cum - logf

        att = [jnp.where(diag_mask, jnp.sum(hd(q, h) * hd(k, h), axis=-1, keepdims=True), 0.0)
               for h in range(HG_HEADS)]
        for l in range(nlev):
            size = 1 << l
            if l == 0:
                qe, ke = q * f, k
            elif size < SMALL:
                qe, ke = q * jnp.exp(msum(lower[l], 2)), k * jnp.exp(msum(upper[l], 2))
            else:
                qe = q * jnp.exp(cum - block_row(cum_ex, size, 0))
                ke = k * jnp.exp(block_row(cum, size, size - 1) - cum)
            qe = qe.astype(BF16)
            ke_t = ke.T.astype(BF16)
            for h in range(HG_HEADS):
                att[h] = att[h] + jnp.where(pair[l], _bdot(hd(qe, h), rows_of(ke_t, h)), 0.0)

        qin = (q * jnp.exp(cum)).astype(BF16)
        seg_tot = block_row(cum, lseg, lseg - 1)
        kend_t = (k * jnp.exp(seg_tot - cum)).T.astype(BF16)
        seg_tot_t = seg_tot.T
        outs, new_states = [], []
        for h in range(HG_HEADS):
            vh = hd(v, h)
            o = _bdot(att[h], vh) + cat([_bdot(seg(hd(qin, h), s), states[h][s]) for s in range(nb)])
            ns = []
            for s in range(nb):
                tot = jnp.broadcast_to(rows_of(seg_tot_t, h)[:, s * lseg:s * lseg + 1], (HG_HEADDIM, HG_HEADDIM))
                kend_s = rows_of(kend_t, h) if nb == 1 else jnp.where(col_seg[s], rows_of(kend_t, h), 0)
                ns.append(jnp.exp(tot) * states[h][s] + _bdot(kend_s, vh))
            new_states.append(ns)
            outs.append(_rms(o, nw_ref[...]) * hd(gate, h))
        _store_rows(y_ref, p, jnp.concatenate(outs, axis=1), nb, lseg)
        for h in range(HG_HEADS):
            for s in range(nb):
                s_ref[p * nb + s, h] = new_states[h][s]

    _for_each_problem(sub, nb, problem)


def _hgrn(x, s0, lb, norm_w, stacked, layer, bsz, seqlen, lseg, nb, sub):
    return _mixer_call(_hg_kernel, "hgrn_mixer", (x,), s0, (lb, norm_w),
                       (_const_in((1, HG_WIDTH)), _const_in((1, HG_HEADDIM))), stacked, layer,
                       HG_WIDTH, (4 * HG_WIDTH,), bsz, seqlen, lseg, nb, sub)


def _rw_kernel(x_ref, s0_ref, w0_ref, wup_ref, a0_ref, aup_ref, gup_ref, kk_ref, ka_ref, rk_ref, lnw_ref, lnb_ref,
               y_ref, s_ref, x_dense, *, lseg, nb, sub):
    _init_states_transposed(s_ref, s0_ref, RW_HEADS, sub)

    R = lseg * nb
    HD, PW, NH, W = RW_HEADDIM, 2 * RW_HEADDIM, RW_HEADS, RW_WIDTH
    assert R == PW
    nlev = int(math.log2(lseg))
    i, j = _iotas(R)
    same = (i >> nlev) == (j >> nlev)
    incl = same & (j <= i)
    strict = same & (j < i)
    m_incl = incl.astype(BF16)
    m_same = same.astype(BF16)
    eye = (i == j).astype(F32)
    pair = [((i >> l) == (j >> l) + 1) & (((i >> l) & 1) == 1) for l in range(nlev)]
    m_head = ((i >> 6) == (j >> 6)).astype(BF16)
    col_seg = [(j[0:1, :] >> nlev) == s for s in range(nb)]
    seg_cols = lambda t, s: t if nb == 1 else jnp.where(col_seg[s], t, 0)
    first_col = jnp.concatenate([(i == s * lseg).astype(BF16) for s in range(nb)], axis=1)
    hd = lambda t, h: t[:, h * HD:(h + 1) * HD]
    rows_of = lambda t, h: t[h * HD:(h + 1) * HD]
    seg = lambda t, s: t[s * lseg:(s + 1) * lseg]
    cat = lambda parts: parts[0] if nb == 1 else jnp.concatenate(parts, axis=0)

    def head_sum(t):
        return jnp.concatenate([_dot_mask(t[:, q * PW:(q + 1) * PW], m_head, 2) for q in range(W // PW)], axis=1)

    def problem(p):
        x = _load_rows(x_ref, p, nb, x_dense)
        states = [[s_ref[p * nb + s, h] for s in range(nb)] for h in range(NH)]
        r, k, v = x[:, 0:W], x[:, W:2 * W], x[:, 2 * W:3 * W]
        wd, ad, gd = x[:, 3 * W:3 * W + 64], x[:, 3 * W + 64:3 * W + 128], x[:, 3 * W + 128:3 * W + 256]
        w = -jax.nn.softplus(-(w0_ref[...] + _bdot(jnp.tanh(wd), wup_ref[...]))) - 0.5
        logw = -jnp.exp(w)
        ag = _sigmoid(a0_ref[...] + _bdot(ad, aup_ref[...]))
        g = _bdot(_sigmoid(gd), gup_ref[...])
        logw_parts = _split(logw, 2)
        rsum = lambda m: (jnp.dot(m, logw_parts[0], preferred_element_type=F32)
                          + jnp.dot(m, logw_parts[1], preferred_element_type=F32))
        b_in = rsum(m_incl)
        b_tot = rsum(m_same)
        e_in = jnp.exp(b_in)
        e_ex = jnp.exp(b_in - logw)

        kk = k * kk_ref[...]
        kk = kk * lax.rsqrt(jnp.maximum(head_sum(kk * kk), 1e-24))
        k2 = k * (1.0 + (ag - 1.0) * ka_ref[...])
        kb = kk * ag
        a_t = (-kk * e_ex).astype(BF16)
        r_t = (r * e_in).astype(BF16)
        vb = v.astype(BF16)
        bonus = head_sum(r * k2 * rk_ref[...]) * v

        b_in_t = b_in.T
        b_tot_t = b_tot.T
        e_neg_t = jnp.exp(-b_in_t)
        e_end_t = jnp.exp(b_tot_t - b_in_t)
        kb_t = kb.T
        k2_t = k2.T
        b_n = (kb_t * e_neg_t).astype(BF16)
        k_n = (k2_t * e_neg_t).astype(BF16)
        b_e = (kb_t * e_end_t).astype(BF16)
        k_e = (k2_t * e_end_t).astype(BF16)
        tot_parts = _split(b_tot_t, 2)

        keys = [jnp.concatenate([rows_of(b_n, h), rows_of(k_n, h)], axis=1) for h in range(NH)]
        am = [_bdot(hd(a_t, h), keys[h]) for h in range(NH)]
        rm = [_bdot(hd(r_t, h), keys[h]) for h in range(NH)]
        a_ab = [jnp.where(strict, am[h][:, :R], 0.0) for h in range(NH)]
        a_ak = [jnp.where(strict, am[h][:, R:], 0.0) for h in range(NH)]
        a_rb = [jnp.where(incl, rm[h][:, :R], 0.0) for h in range(NH)]
        a_rk = [jnp.where(incl, rm[h][:, R:], 0.0) for h in range(NH)]
        zmat = [cat([_bdot(seg(hd(a_t, h), s), states[h][s]) for s in range(nb)]) + _bdot(a_ak[h], hd(vb, h))
                for h in range(NH)]
        y0 = [cat([_bdot(seg(hd(r_t, h), s), states[h][s]) for s in range(nb)]) + _bdot(a_rk[h], hd(vb, h))
              for h in range(NH)]
        inv = [eye + jnp.where(pair[0], a_ab[h], 0.0) for h in range(NH)]
        for l in range(1, nlev):
            tmp = [_bdot(jnp.where(pair[l], a_ab[h], 0.0), inv[h]) for h in range(NH)]
            inv = [inv[h] + _bdot(inv[h], tmp[h]) for h in range(NH)]
        sa = [_bdot(inv[h], zmat[h]) for h in range(NH)]
        ys = [y0[h] + _bdot(a_rb[h], sa[h]) for h in range(NH)]
        e_tot = [jnp.exp(jnp.dot(rows_of(tot_parts[0], h), first_col, preferred_element_type=F32)
                         + jnp.dot(rows_of(tot_parts[1], h), first_col, preferred_element_type=F32))
                 for h in range(NH)]
        new_states = [[states[h][s] * e_tot[h][:, s * PW:s * PW + HD]
                       + _bdot(seg_cols(rows_of(b_e, h), s), sa[h]) + _bdot(seg_cols(rows_of(k_e, h), s), hd(vb, h))
                       for s in range(nb)] for h in range(NH)]
        y = jnp.concatenate(ys, axis=1)
        mu = head_sum(y) * (1.0 / RW_HEADDIM)
        yc = y - mu
        var = head_sum(yc * yc) * (1.0 / RW_HEADDIM)
        y = yc * lax.rsqrt(var + RW_LN_EPS) * lnw_ref[...] + lnb_ref[...] + bonus
        _store_rows(y_ref, p, y * g, nb, lseg)
        for h in range(NH):
            for s in range(nb):
                s_ref[p * nb + s, h] = new_states[h][s]

    _for_each_problem(sub, nb, problem)
    _finish_states_transposed(s_ref, RW_HEADS, sub)


def _rwkv(x, s0, p, stacked, layer, bsz, seqlen, lseg, nb, sub):
    vec = _const_in((1, RW_WIDTH))
    return _mixer_call(_rw_kernel, "rwkv_mixer", (x,), s0,
                       (p['w0'], p['w_up'], p['a0'], p['a_up'], p['g_up'], p['k_k'], p['k_a'], p['r_k'], p['ln_w'],
                        p['ln_b']),
                       (vec, _const_in((64, RW_WIDTH)), vec, _const_in((64, RW_WIDTH)), _const_in((128, RW_WIDTH)),
                        vec, vec, vec, vec, vec), stacked, layer,
                       RW_WIDTH, (RW_PROJ,), bsz, seqlen, lseg, nb, sub)


def _merge_kernel(x_ref, yssd_ref, ys5_ref, yhg_ref, yrw_ref, nw_ref, wm_ref, bm_ref,
                  wssd_ref, ws5_ref, whg_ref, wrw_ref, wout_ref, o_ref):
    x = x_ref[...]
    xn = _rms(x, nw_ref[...]).astype(BF16)
    merged = None
    for b, (y_ref, w_ref) in enumerate(((yssd_ref, wssd_ref), (ys5_ref, ws5_ref), (yhg_ref, whg_ref), (yrw_ref, wrw_ref))):
        cs = slice(b * D_MODEL, (b + 1) * D_MODEL)
        gate = _sigmoid(jnp.dot(xn, wm_ref[:, cs], preferred_element_type=F32) + bm_ref[:, cs])
        t = gate * jnp.dot(y_ref[...].astype(BF16), w_ref[...], preferred_element_type=F32)
        merged = t if merged is None else merged + t
    o_ref[...] = x + jnp.dot(merged.astype(BF16), wout_ref[...], preferred_element_type=F32)


def _merge(x, y_ssd, y_s5, y_hg, y_rw, tb, norm_w, w_merge, b_merge, w_ssd, w_s5, w_hg, w_rw, w_out):
    rows = x.shape[0]
    row = lambda w: pl.BlockSpec((tb, w), lambda i: (i, 0))
    return pl.pallas_call(
        _merge_kernel, grid=(rows // tb,),
        in_specs=[row(D_MODEL), row(1024), row(S5_WIDTH), row(HG_WIDTH), row(RW_WIDTH),
                  _const_in((1, D_MODEL)), _const_in((D_MODEL, 4 * D_MODEL)), _const_in((1, 4 * D_MODEL)),
                  _const_in((1024, D_MODEL)), _const_in((S5_WIDTH, D_MODEL)), _const_in((HG_WIDTH, D_MODEL)),
                  _const_in((RW_WIDTH, D_MODEL)), _const_in((D_MODEL, D_MODEL))],
        out_specs=row(D_MODEL), out_shape=jax.ShapeDtypeStruct((rows, D_MODEL), F32),
        compiler_params=_params(("arbitrary",)), name="merge",
    )(x, y_ssd, y_s5, y_hg, y_rw, norm_w, w_merge, b_merge, w_ssd, w_s5, w_hg, w_rw, w_out)


def _ffn_kernel(x_ref, nw_ref, wup_ref, cw_ref, cb_ref, wdn_ref, conv0_ref, fnw_ref, o_ref, convst_ref, full_ref,
                *, bsz, tb, final_norm):
    lookback = (FFN_CONV - 1) * bsz

    @pl.when(pl.program_id(0) == 0)
    def _():
        full_ref[0:lookback, :] = conv0_ref[...]

    x = x_ref[...]
    xn = _rms(x, nw_ref[...]).astype(BF16)
    full_ref[lookback:lookback + tb, :] = jnp.dot(xn, wup_ref[...], preferred_element_type=F32)
    acc = cb_ref[...] + full_ref[0:tb, :] * cw_ref[0:1, :]
    for j in range(1, FFN_CONV):
        acc = acc + full_ref[j * bsz:j * bsz + tb, :] * cw_ref[j:j + 1, :]
    _shift_rows_down(full_ref, tb, lookback)
    convst_ref[...] = full_ref[0:lookback, :]
    hidden = jax.nn.gelu(acc[:, 0:D_FF]) * acc[:, D_FF:2 * D_FF]
    out = x + jnp.dot(hidden.astype(BF16), wdn_ref[...], preferred_element_type=F32)
    if final_norm:
        out = _rms(out, fnw_ref[...])
    o_ref[...] = out


def _ffn(x, bsz, tb, norm_w, w_up, conv_w, conv_b, w_down, conv0, final_w, final_norm):
    rows = x.shape[0]
    lookback = (FFN_CONV - 1) * bsz
    assert rows % tb == 0 and tb % bsz == 0
    row = pl.BlockSpec((tb, D_MODEL), lambda i: (i, 0))
    return pl.pallas_call(
        functools.partial(_ffn_kernel, bsz=bsz, tb=tb, final_norm=final_norm),
        grid=(rows // tb,),
        in_specs=[row, _const_in((1, D_MODEL)), _const_in((D_MODEL, 2 * D_FF)), _const_in((FFN_CONV, 2 * D_FF)),
                  _const_in((1, 2 * D_FF)), _const_in((D_FF, D_MODEL)), _const_in((lookback, 2 * D_FF)),
                  _const_in((1, D_MODEL))],
        out_specs=(row, _const_spec((lookback, 2 * D_FF))),
        out_shape=(jax.ShapeDtypeStruct((rows, D_MODEL), F32), jax.ShapeDtypeStruct((lookback, 2 * D_FF), F32)),
        scratch_shapes=[pltpu.VMEM((lookback + tb, 2 * D_FF), F32)],
        compiler_params=_params(("arbitrary",)), name="conv_ffn",
    )(x, norm_w, w_up, conv_w, conv_b, w_down, conv0, final_w)


def _pad_lanes(v, width=DT_PAD):
    return jnp.pad(v, (0, width - v.shape[0])).reshape(1, width)


def _layer_params(l, P, lb_all):
    p = {n: a[l] for n, a in P.items()}
    row = lambda a: a.reshape(1, -1)
    ab_re, ab_im, s5_win, s5_wcre, s5_wcim = _s5_prep(p['s5_log_dt'], p['s5_a_re'], p['s5_a_im'], p['s5_b_re'],
                                                      p['s5_b_im'], p['s5_c_re'], p['s5_c_im'])
    q = dict(
        norm1_w=row(p['norm1_w']), w_in=_pack_w_in(p['w_in']),
        ssd_conv_w=p['ssd_conv_w'], ssd_conv_b=row(p['ssd_conv_b']), ssd_dt_bias=_pad_lanes(p['ssd_dt_bias']),
        ssd_a_log=_pad_lanes(p['ssd_a_log']), ssd_d=_pad_lanes(p['ssd_d']), ssd_norm_w=row(p['ssd_norm_w']),
        s5_win=s5_win, s5_wcre=s5_wcre, s5_wcim=s5_wcim,
        s5_ab_re=row(ab_re), s5_ab_im=row(ab_im), s5_d=row(p['s5_d']),
        s5_glu_w=p['s5_glu_w'].astype(BF16), s5_glu_b=row(p['s5_glu_b']),
        hg_lb=lb_all[l:l + 1], hg_norm_w=row(p['hg_norm_w']),
        rw_mu=row(p['rw_mu']),
        rw=dict(w0=row(p['rw_w0']), w_up=p['rw_w_up'].astype(BF16), a0=row(p['rw_a0']), a_up=p['rw_a_up'].astype(BF16),
                g_up=p['rw_g_up'].astype(BF16), k_k=row(p['rw_k_k']), k_a=row(p['rw_k_a']), r_k=row(p['rw_r_k']),
                ln_w=row(p['rw_ln_w']), ln_b=row(p['rw_ln_b'])),
        w_merge=p['w_merge'].astype(BF16), b_merge=row(p['b_merge']),
        w_br_ssd=p['w_br_ssd'].astype(BF16), w_br_s5=p['w_br_s5'].astype(BF16), w_br_hg=p['w_br_hg'].astype(BF16),
        w_br_rw=p['w_br_rw'].astype(BF16), w_out=p['w_out'].astype(BF16),
        norm2_w=row(p['norm2_w']), ffn_up=p['ffn_up'].astype(BF16), ffn_conv_w=p['ffn_conv_w'],
        ffn_conv_b=row(p['ffn_conv_b']), ffn_down=p['ffn_down'].astype(BF16),
    )
    return q


class _Group:
    def __init__(self, bsz, seqlen):
        self.bsz, self.seqlen = bsz, seqlen
        rows = bsz * seqlen
        self.tb = min(rows, max(256, bsz))
        self.tt = self.tb // bsz
        if seqlen >= 128:
            self.ssd, self.hg, self.rw = (128, 2, 8), (64, 2, 8), (64, 2, 8)
        else:
            self.ssd = self.hg = self.rw = (seqlen, 128 // seqlen, 128 // seqlen)


def _time_major(a):
    a = jnp.swapaxes(a, 0, 1)
    return a.reshape((a.shape[0] * a.shape[1],) + a.shape[2:])


def _batch_major(a, bsz):
    a = a.reshape((a.shape[0] // bsz, bsz) + a.shape[1:])
    return jnp.swapaxes(a, 0, 1)


def _trunk(x, states, layers, final_norm_w, grp):
    bsz, seqlen = grp.bsz, grp.seqlen
    xt = _time_major(x)
    small_states = []
    ssd_all = hg_all = rw_all = None
    for l, q in enumerate(layers):
        s_ssd, s_conv, s_s5r, s_s5i, s_hg, s_rw, s_shift, s_fconv = states[l]
        z, xbc, dt, u, hg, rwx, conv_new, shift_new = _in_proj(
            xt, bsz, grp.tb, q['norm1_w'], q['w_in'], q['ssd_conv_w'], q['ssd_conv_b'], q['ssd_dt_bias'], q['rw_mu'],
            _time_major(s_conv), s_shift)
        y_ssd, ssd_all = _ssd(z, xbc, dt, s_ssd, q['ssd_a_log'], q['ssd_d'], q['ssd_norm_w'], ssd_all, l,
                              bsz, seqlen, *grp.ssd)
        y_s5, hr_new, hi_new = _s5(u, bsz, grp.tt, q['s5_win'], q['s5_wcre'], q['s5_wcim'], q['s5_ab_re'], q['s5_ab_im'],
                                   q['s5_d'], q['s5_glu_w'], q['s5_glu_b'],
                                   s_s5r.reshape(bsz, S5_CH), s_s5i.reshape(bsz, S5_CH))
        y_hg, hg_all = _hgrn(hg, s_hg, q['hg_lb'], q['hg_norm_w'], hg_all, l, bsz, seqlen, *grp.hg)
        y_rw, rw_all = _rwkv(rwx, s_rw, q['rw'], rw_all, l, bsz, seqlen, *grp.rw)
        x1 = _merge(xt, y_ssd, y_s5, y_hg, y_rw, grp.tb, q['norm1_w'], q['w_merge'], q['b_merge'],
                    q['w_br_ssd'], q['w_br_s5'], q['w_br_hg'], q['w_br_rw'], q['w_out'])
        xt, fconv_new = _ffn(x1, bsz, grp.tb, q['norm2_w'], q['ffn_up'], q['ffn_conv_w'], q['ffn_conv_b'],
                             q['ffn_down'], _time_major(s_fconv), final_norm_w.reshape(1, D_MODEL), l == DEPTH - 1)
        small_states.append((_batch_major(conv_new, bsz),
                             hr_new.reshape(bsz, S5_GROUPS, S5_STATE), hi_new.reshape(bsz, S5_GROUPS, S5_STATE),
                             shift_new, _batch_major(fconv_new, bsz)))
    conv_all, s5r_all, s5i_all, shift_all, fconv_all = (
        jnp.stack([st[k] for st in small_states], axis=0) for k in range(5))
    return _batch_major(xt, bsz), (ssd_all, conv_all, s5r_all, s5i_all, hg_all, rw_all, shift_all, fconv_all)


def _zero_states(bsz):
    z = lambda *s: jnp.zeros((bsz,) + s, F32)
    return (z(SSD_HEADS, SSD_HEADDIM, SSD_D_STATE), z(SSD_CONV - 1, SSD_CONV_CH), z(S5_GROUPS, S5_STATE),
            z(S5_GROUPS, S5_STATE), z(HG_HEADS, HG_HEADDIM, HG_HEADDIM), z(RW_HEADS, RW_HEADDIM, RW_HEADDIM),
            z(RW_PROJ), z(FFN_CONV - 1, 2 * D_FF))


def kernel(x_prompt, x_sample, state_ssd, state_ssd_conv, state_s5_re, state_s5_im, state_hgrn, state_rwkv, state_rwkv_shift, state_ffn_conv, norm1_w, w_in, ssd_conv_w, ssd_conv_b, ssd_dt_bias, ssd_a_log, ssd_d, ssd_norm_w, s5_a_re, s5_a_im, s5_log_dt, s5_b_re, s5_b_im, s5_c_re, s5_c_im, s5_d, s5_glu_w, s5_glu_b, hg_lb_raw, hg_norm_w, rw_mu, rw_w0, rw_w_up, rw_a0, rw_a_up, rw_g_up, rw_k_k, rw_k_a, rw_r_k, rw_ln_w, rw_ln_b, w_br_ssd, w_br_s5, w_br_hg, w_br_rw, w_merge, b_merge, w_out, norm2_w, ffn_up, ffn_conv_w, ffn_conv_b, ffn_down, final_norm_w):
    P = dict(norm1_w=norm1_w, w_in=w_in, ssd_conv_w=ssd_conv_w, ssd_conv_b=ssd_conv_b, ssd_dt_bias=ssd_dt_bias,
             ssd_a_log=ssd_a_log, ssd_d=ssd_d, ssd_norm_w=ssd_norm_w, s5_a_re=s5_a_re, s5_a_im=s5_a_im,
             s5_log_dt=s5_log_dt, s5_b_re=s5_b_re, s5_b_im=s5_b_im, s5_c_re=s5_c_re, s5_c_im=s5_c_im, s5_d=s5_d,
             s5_glu_w=s5_glu_w, s5_glu_b=s5_glu_b, hg_norm_w=hg_norm_w, rw_mu=rw_mu, rw_w0=rw_w0, rw_w_up=rw_w_up,
             rw_a0=rw_a0, rw_a_up=rw_a_up, rw_g_up=rw_g_up,
             rw_k_k=rw_k_k.reshape(DEPTH, RW_WIDTH), rw_k_a=rw_k_a.reshape(DEPTH, RW_WIDTH),
             rw_r_k=rw_r_k.reshape(DEPTH, RW_WIDTH), rw_ln_w=rw_ln_w.reshape(DEPTH, RW_WIDTH),
             rw_ln_b=rw_ln_b.reshape(DEPTH, RW_WIDTH),
             w_br_ssd=w_br_ssd, w_br_s5=w_br_s5, w_br_hg=w_br_hg, w_br_rw=w_br_rw, w_merge=w_merge, b_merge=b_merge,
             w_out=w_out, norm2_w=norm2_w, ffn_up=ffn_up, ffn_conv_w=ffn_conv_w, ffn_conv_b=ffn_conv_b,
             ffn_down=ffn_down)
    lb_all = _lower_bounds(hg_lb_raw)
    layers = [_layer_params(l, P, lb_all) for l in range(DEPTH)]
    sample_states = (state_ssd, state_ssd_conv, state_s5_re, state_s5_im, state_hgrn, state_rwkv,
                     state_rwkv_shift, state_ffn_conv)
    sample_init = [tuple(s[l] for s in sample_states) for l in range(DEPTH)]
    prompt_init = [_zero_states(x_prompt.shape[0])] * DEPTH
    y_prompt, p_states = _trunk(x_prompt, prompt_init, layers, final_norm_w, _Group(*x_prompt.shape[:2]))
    y_sample, s_states = _trunk(x_sample, sample_init, layers, final_norm_w, _Group(*x_sample.shape[:2]))
    return (y_prompt, y_sample) + p_states + s_states
```

```python
import functools
import math

import jax
import jax.numpy as jnp
from jax import lax
from jax.experimental import pallas as pl
from jax.experimental.pallas import tpu as pltpu

F32 = jnp.float32
BF16 = jnp.bfloat16

D_MODEL = 1024
DEPTH = 4
SSD_HEADS = 16
SSD_HEADDIM = 64
SSD_D_STATE = 64
SSD_GROUPS = 4
SSD_CONV = 4
SSD_CONV_CH = 1536
S5_WIDTH = 512
S5_GROUPS = 32
S5_GROUP = 16
S5_STATE = 64
S5_CH = S5_GROUPS * S5_STATE
HG_WIDTH = 512
HG_HEADS = 4
HG_HEADDIM = 128
RW_WIDTH = 512
RW_HEADS = 8
RW_HEADDIM = 64
RW_PROJ = 1792
RW_LN_EPS = 64e-5
D_FF = 2816
FFN_CONV = 3
EPS = 1e-6
IN_SIZES = (1024, 1536, 16, 512, 512, 512, 512, 512, 1792)
DT_PAD = 128

VMEM_LIMIT = 56 * 1024 * 1024


def _bdot(a, b):
    return jnp.dot(a.astype(BF16), b.astype(BF16), preferred_element_type=F32)


def _split(x, n):
    parts = []
    r = x
    for _ in range(n):
        p = r.astype(BF16)
        parts.append(p)
        r = r - p.astype(F32)
    return parts


def _mask_dot(m, x, n=3):
    out = None
    for p in _split(x, n):
        t = jnp.dot(m, p, preferred_element_type=F32)
        out = t if out is None else out + t
    return out


def _dot_mask(x, m, n=3):
    out = None
    for p in _split(x, n):
        t = jnp.dot(p, m, preferred_element_type=F32)
        out = t if out is None else out + t
    return out


def _rms(x, w):
    return x * lax.rsqrt(jnp.mean(x * x, axis=-1, keepdims=True) + EPS) * w


def _sigmoid(x):
    return jax.nn.sigmoid(x)


def _silu(x):
    return x * jax.nn.sigmoid(x)


def _load_rows(ref, p, nb, dense_ref):
    lseg = ref.shape[0]
    for s in range(nb):
        dense_ref[s * lseg:(s + 1) * lseg, :] = ref[:, p * nb + s, :]
    return dense_ref[...]


def _store_rows(ref, p, y, nb, lseg):
    for s in range(nb):
        ref[:, p * nb + s, :] = y[s * lseg:(s + 1) * lseg, :]


def _init_states_transposed(s_ref, s0_ref, nheads, sub):
    @pl.when(pl.program_id(1) == 0)
    def _():
        def body(b, carry):
            for h in range(nheads):
                s_ref[b, h] = s0_ref[b, h].T
            return carry
        lax.fori_loop(0, sub, body, 0)


def _finish_states_transposed(s_ref, nheads, sub):
    @pl.when(pl.program_id(1) == pl.num_programs(1) - 1)
    def _():
        def body(b, carry):
            for h in range(nheads):
                s_ref[b, h] = s_ref[b, h].T
            return carry
        lax.fori_loop(0, sub, body, 0)


def _for_each_problem(sub, nb, problem):
    if nb == sub:
        problem(0)
    else:
        def body(p, carry):
            problem(p)
            return carry
        lax.fori_loop(0, sub // nb, body, 0)


def _shift_rows_down(ref, dist, n):
    for off in range(0, n, dist):
        m = min(dist, n - off)
        ref[off:off + m, :] = ref[off + dist:off + dist + m, :]


def _iotas(rows):
    i = lax.broadcasted_iota(jnp.int32, (rows, rows), 0)
    j = lax.broadcasted_iota(jnp.int32, (rows, rows), 1)
    return i, j


def _const_spec(shape):
    nd = len(shape)
    return pl.BlockSpec(shape, lambda *_: (0,) * nd)


def _const_in(shape):
    nd = len(shape)
    return pl.BlockSpec(shape, lambda *_: (0,) * nd, pipeline_mode=pl.Buffered(1))


def _params(sem):
    return pltpu.CompilerParams(dimension_semantics=sem, vmem_limit_bytes=VMEM_LIMIT)


def _lb_kernel(raw_ref, o_ref):
    raw = raw_ref[...]
    m = jnp.max(raw, axis=0, keepdims=True)
    e = jnp.exp(raw - m)
    sm = e / jnp.sum(e, axis=0, keepdims=True)
    acc = jnp.zeros_like(sm[0:1])
    rows = []
    for l in range(DEPTH):
        acc = acc + sm[l:l + 1]
        rows.append(acc)
    first = rows[0]
    o_ref[...] = jnp.concatenate([r - first for r in rows], axis=0)


def _lower_bounds(hg_lb_raw):
    return pl.pallas_call(_lb_kernel, out_shape=jax.ShapeDtypeStruct((DEPTH, HG_WIDTH), F32),
                          name="hg_lower_bounds")(hg_lb_raw)


S5_PACK = 8
S5_SLABS = S5_GROUPS // S5_PACK
S5_SLAB_IN = S5_PACK * S5_GROUP
S5_SLAB_CH = S5_PACK * S5_STATE


def _s5_prep_kernel(ldt_ref, are_ref, aim_ref, bre_ref, bim_ref, cre_ref, cim_ref,
                    abre_ref, abim_ref, win_ref, wcre_ref, wcim_ref):
    dt = jnp.exp(ldt_ref[...])
    a_re = are_ref[...]
    a_im = aim_ref[...]
    mag = jnp.exp(dt * a_re)
    ab_re = mag * jnp.cos(dt * a_im)
    ab_im = mag * jnp.sin(dt * a_im)
    den = a_re * a_re + a_im * a_im
    q_re = ((ab_re - 1.0) * a_re + ab_im * a_im) / den
    q_im = (ab_im * a_re - (ab_re - 1.0) * a_im) / den
    abre_ref[...] = ab_re
    abim_ref[...] = ab_im
    win_ref[...] = jnp.zeros(win_ref.shape, win_ref.dtype)
    wcre_ref[...] = jnp.zeros(wcre_ref.shape, wcre_ref.dtype)
    wcim_ref[...] = jnp.zeros(wcim_ref.shape, wcim_ref.dtype)
    for g in range(S5_GROUPS):
        slab, r = divmod(g, S5_PACK)
        rows_in = slice(r * S5_GROUP, (r + 1) * S5_GROUP)
        ch = slice(r * S5_STATE, (r + 1) * S5_STATE)
        ch_im = slice(S5_SLAB_CH + r * S5_STATE, S5_SLAB_CH + (r + 1) * S5_STATE)
        b_re = bre_ref[g * S5_GROUP:(g + 1) * S5_GROUP, :]
        b_im = bim_ref[g * S5_GROUP:(g + 1) * S5_GROUP, :]
        win_ref[slab, rows_in, ch] = (q_re[g:g + 1] * b_re - q_im[g:g + 1] * b_im).astype(win_ref.dtype)
        win_ref[slab, rows_in, ch_im] = (q_re[g:g + 1] * b_im + q_im[g:g + 1] * b_re).astype(win_ref.dtype)
        wcre_ref[slab, ch, rows_in] = cre_ref[g * S5_STATE:(g + 1) * S5_STATE, :].astype(wcre_ref.dtype)
        wcim_ref[slab, ch, rows_in] = cim_ref[g * S5_STATE:(g + 1) * S5_STATE, :].astype(wcim_ref.dtype)


def _s5_prep(log_dt, a_re, a_im, b_re, b_im, c_re, c_im):
    bt_re = jnp.swapaxes(b_re, 1, 2).reshape(S5_WIDTH, S5_STATE)
    bt_im = jnp.swapaxes(b_im, 1, 2).reshape(S5_WIDTH, S5_STATE)
    ct_re = jnp.swapaxes(c_re, 1, 2).reshape(S5_CH, S5_GROUP)
    ct_im = jnp.swapaxes(c_im, 1, 2).reshape(S5_CH, S5_GROUP)
    gn = jax.ShapeDtypeStruct((S5_GROUPS, S5_STATE), F32)
    return pl.pallas_call(
        _s5_prep_kernel,
        out_shape=(gn, gn, jax.ShapeDtypeStruct((S5_SLABS, S5_SLAB_IN, 2 * S5_SLAB_CH), BF16),
                   jax.ShapeDtypeStruct((S5_SLABS, S5_SLAB_CH, S5_SLAB_IN), BF16),
                   jax.ShapeDtypeStruct((S5_SLABS, S5_SLAB_CH, S5_SLAB_IN), BF16)),
        name="s5_discretise")(log_dt.reshape(S5_GROUPS, 1), a_re, a_im, bt_re, bt_im, ct_re, ct_im)


SEG_Z = (0, 1024)
SEG_XBC = (1024, 1536)
SEG_U = (2560, 512)
SEG_HG = (3072, 2048)
SEG_RW = (5120, 1792)
SEG_DT = (6912, DT_PAD)
IN_PACKED = 7040


def _pack_w_in(w_in):
    off = [0]
    for s in IN_SIZES:
        off.append(off[-1] + s)
    z, xbc, dt, u, q, f, i, g, rw = (w_in[:, off[k]:off[k + 1]] for k in range(9))
    dt = jnp.pad(dt, ((0, 0), (0, DT_PAD - dt.shape[1])))
    return jnp.concatenate([z, xbc, u, q, f, i, g, rw, dt], axis=1).astype(BF16)


def _in_kernel(x_ref, nw_ref, w_ref, cw_ref, cb_ref, dtb_ref, mu_ref, conv0_ref, sh0_ref,
               z_ref, xbc_ref, dt_ref, u_ref, hg_ref, rw_ref, convst_ref, shst_ref,
               full_ref, rwfull_ref, *, bsz, tb):
    step = pl.program_id(0)
    lookback = (SSD_CONV - 1) * bsz

    @pl.when(step == 0)
    def _():
        full_ref[0:lookback, :] = conv0_ref[...]
        rwfull_ref[0:bsz, :] = sh0_ref[...]

    xn = _rms(x_ref[...], nw_ref[...]).astype(BF16)

    def proj(seg):
        return jnp.dot(xn, w_ref[:, seg[0]:seg[0] + seg[1]], preferred_element_type=F32)

    z_ref[...] = proj(SEG_Z)
    u_ref[...] = proj(SEG_U)
    hg_ref[...] = proj(SEG_HG)
    dt_ref[...] = jax.nn.softplus(proj(SEG_DT) + dtb_ref[...])

    full_ref[lookback:lookback + tb, :] = proj(SEG_XBC)
    acc = cb_ref[...] + full_ref[0:tb, :] * cw_ref[0:1, :]
    for j in range(1, SSD_CONV):
        acc = acc + full_ref[j * bsz:j * bsz + tb, :] * cw_ref[j:j + 1, :]
    xbc_ref[...] = _silu(acc)
    _shift_rows_down(full_ref, tb, lookback)
    convst_ref[...] = full_ref[0:lookback, :]

    rwfull_ref[bsz:bsz + tb, :] = proj(SEG_RW)
    cur = rwfull_ref[bsz:bsz + tb, :]
    prev = rwfull_ref[0:tb, :]
    rw_ref[...] = cur + (prev - cur) * mu_ref[...]
    last = rwfull_ref[tb:tb + bsz, :]
    rwfull_ref[0:bsz, :] = last
    shst_ref[...] = last


def _in_proj(x, bsz, tb, norm_w, w_packed, conv_w, conv_b, dt_bias, mu, conv0, shift0):
    rows = x.shape[0]
    lookback = (SSD_CONV - 1) * bsz
    assert rows % tb == 0 and tb % bsz == 0
    row = lambda w: pl.BlockSpec((tb, w), lambda i: (i, 0))
    widths = (1024, SSD_CONV_CH, DT_PAD, S5_WIDTH, 4 * HG_WIDTH, RW_PROJ)
    out_shape = tuple(jax.ShapeDtypeStruct((rows, w), F32) for w in widths) + (
        jax.ShapeDtypeStruct((lookback, SSD_CONV_CH), F32), jax.ShapeDtypeStruct((bsz, RW_PROJ), F32))
    out_specs = tuple(row(w) for w in widths) + (_const_spec((lookback, SSD_CONV_CH)), _const_spec((bsz, RW_PROJ)))
    in_specs = [row(D_MODEL), _const_in((1, D_MODEL)), _const_in((D_MODEL, IN_PACKED)),
                _const_in((SSD_CONV, SSD_CONV_CH)), _const_in((1, SSD_CONV_CH)), _const_in((1, DT_PAD)),
                _const_in((1, RW_PROJ)), _const_in((lookback, SSD_CONV_CH)), _const_in((bsz, RW_PROJ))]
    return pl.pallas_call(
        functools.partial(_in_kernel, bsz=bsz, tb=tb),
        grid=(rows // tb,), in_specs=in_specs, out_specs=out_specs, out_shape=out_shape,
        scratch_shapes=[pltpu.VMEM((lookback + tb, SSD_CONV_CH), F32), pltpu.VMEM((bsz + tb, RW_PROJ), F32)],
        compiler_params=_params(("arbitrary",)), name="in_proj",
    )(x, norm_w, w_packed, conv_w, conv_b, dt_bias, mu, conv0, shift0)


def _ssd_kernel(z_ref, xbc_ref, dt_ref, s0_ref, alog_ref, d_ref, nw_ref, y_ref, s_ref,
                z_dense, xbc_dense, dt_dense, *, lseg, nb, sub):
    _init_states_transposed(s_ref, s0_ref, SSD_HEADS, sub)

    rows = lseg * nb
    sh = int(math.log2(lseg))
    n_x = SSD_HEADS * SSD_HEADDIM
    gw = SSD_GROUPS * SSD_D_STATE
    hpg = SSD_HEADS // SSD_GROUPS
    neg_a = -jnp.exp(alog_ref[...])
    d_skip = d_ref[...]
    i, j = _iotas(rows)
    same = (i >> sh) == (j >> sh)
    tril = same & (j <= i)
    m_tril = tril.astype(BF16)
    m_triu = (same & (i <= j)).astype(BF16)
    m_same = same.astype(BF16)
    seg = lambda t, s: t[s * lseg:(s + 1) * lseg]
    cat = lambda parts: parts[0] if nb == 1 else jnp.concatenate(parts, axis=0)

    col_seg = [(j[0:1, :] >> sh) == s for s in range(nb)]
    head_lanes = (lax.broadcasted_iota(jnp.int32, (DT_PAD, n_x), 0)
                  == (lax.broadcasted_iota(jnp.int32, (DT_PAD, n_x), 1) >> 6)).astype(BF16)

    def problem(p):
        z = _load_rows(z_ref, p, nb, z_dense)
        xbc = _load_rows(xbc_ref, p, nb, xbc_dense)
        dt = _load_rows(dt_ref, p, nb, dt_dense)
        states = [[s_ref[p * nb + s, h] for s in range(nb)] for h in range(SSD_HEADS)]
        a = dt * neg_a
        cum = _mask_dot(m_tril, a, 2)
        cum_t = _dot_mask(a.T, m_triu, 2)
        tot = _mask_dot(m_same, a, 2)
        dec_end = jnp.exp(tot - cum)
        dec_in = jnp.exp(cum)
        dec_tot = jnp.exp(tot)
        bm_t = xbc[:, n_x:n_x + gw].T.astype(BF16)
        bts = [bm_t[g * SSD_D_STATE:(g + 1) * SSD_D_STATE] for g in range(SSD_GROUPS)]
        if nb > 1:
            zero = jnp.zeros_like(bts[0])
            bts_seg = [[jnp.where(col_seg[s], bts[g], zero) for s in range(nb)] for g in range(SSD_GROUPS)]
        else:
            bts_seg = [[bts[g]] for g in range(SSD_GROUPS)]
        cms = [xbc[:, n_x + gw + g * SSD_D_STATE:n_x + gw + (g + 1) * SSD_D_STATE].astype(BF16)
               for g in range(SSD_GROUPS)]
        gmats = [_bdot(cms[g], bts[g]) for g in range(SSD_GROUPS)]
        xs = xbc[:, 0:n_x]
        xdt_all = xs * _dot_mask(dt, head_lanes, 2)
        xdec_all = (xdt_all * _dot_mask(dec_end, head_lanes, 2)).astype(BF16)
        xdt_all = xdt_all.astype(BF16)
        ys, offs, new_states = [], [], []
        for h in range(SSD_HEADS):
            g = h // hpg
            hs = slice(h * SSD_HEADDIM, (h + 1) * SSD_HEADDIM)
            diff = cum[:, h:h + 1] - cum_t[h:h + 1, :]
            lmat = jnp.where(tril, jnp.exp(jnp.minimum(diff, 0.0)), 0.0)
            ys.append(_bdot(gmats[g] * lmat, xdt_all[:, hs]))
            offs.append(cat([_bdot(seg(cms[g], s), states[h][s]) for s in range(nb)]))
            new_states.append([states[h][s] * dec_tot[s * lseg:s * lseg + 1, h:h + 1]
                               + _bdot(bts_seg[g][s], xdec_all[:, hs]) for s in range(nb)])
        yall = (jnp.concatenate(ys, axis=1) + jnp.concatenate(offs, axis=1) * _dot_mask(dec_in, head_lanes, 2)
                + xs * _dot_mask(d_skip, head_lanes, 2))
        yall = _rms(yall * _silu(z), nw_ref[...])
        _store_rows(y_ref, p, yall, nb, lseg)
        for h in range(SSD_HEADS):
            for s in range(nb):
                s_ref[p * nb + s, h] = new_states[h][s]

    _for_each_problem(sub, nb, problem)
    _finish_states_transposed(s_ref, SSD_HEADS, sub)


def _without_ref(fn, idx):
    def body(*refs):
        return fn(*refs[:idx], *refs[idx + 1:])
    return body


def _mixer_call(kernel_fn, name, xs, s0, consts, const_specs, stacked, layer, width_out, scratch_widths,
                bsz, seqlen, lseg, nb, sub):
    blk = lambda w: pl.BlockSpec((lseg, sub, w), lambda b, c: (c, b, 0))
    zeros = (0,) * (s0.ndim - 1)
    st_in = pl.BlockSpec((sub,) + s0.shape[1:], lambda b, c: (b,) + zeros)
    st_out = pl.BlockSpec((None, sub) + s0.shape[1:], lambda b, c: (layer, b) + zeros)
    in_specs = [blk(x.shape[1]) for x in xs] + [st_in] + list(const_specs)
    args = [x.reshape(seqlen, bsz, x.shape[1]) for x in xs] + [s0] + list(consts)
    body = functools.partial(kernel_fn, lseg=lseg, nb=nb, sub=sub)
    aliases = {}
    if stacked is not None:
        aliases = {len(args): 1}
        body = _without_ref(body, len(args))
        in_specs.append(pl.BlockSpec(memory_space=pl.ANY))
        args.append(stacked)
    y, s = pl.pallas_call(
        body, grid=(bsz // sub, seqlen // lseg), in_specs=in_specs, out_specs=(blk(width_out), st_out),
        out_shape=(jax.ShapeDtypeStruct((seqlen, bsz, width_out), F32),
                   jax.ShapeDtypeStruct((DEPTH,) + s0.shape, F32)),
        scratch_shapes=[pltpu.VMEM((nb * lseg, w), F32) for w in scratch_widths],
        input_output_aliases=aliases, compiler_params=_params(("arbitrary", "arbitrary")), name=name,
    )(*args)
    return y.reshape(seqlen * bsz, width_out), s


def _ssd(z, xbc, dt, s0, a_log, d_skip, norm_w, stacked, layer, bsz, seqlen, lseg, nb, sub):
    n_x = SSD_HEADS * SSD_HEADDIM
    return _mixer_call(_ssd_kernel, "ssd_mixer", (z, xbc, dt), s0, (a_log, d_skip, norm_w),
                       (_const_in((1, DT_PAD)), _const_in((1, DT_PAD)), _const_in((1, n_x))), stacked, layer,
                       n_x, (n_x, SSD_CONV_CH, DT_PAD), bsz, seqlen, lseg, nb, sub)


def _s5_kernel(u_ref, win_ref, wcre_ref, wcim_ref, abre_ref, abim_ref, d_ref, gw_ref, gb_ref, hr0_ref, hi0_ref,
               y_ref, hr_ref, hi_ref, bu_ref, *, bsz, tt):
    @pl.when(pl.program_id(0) == 0)
    def _():
        hr_ref[...] = hr0_ref[...]
        hi_ref[...] = hi0_ref[...]

    u = u_ref[...]
    ub = u.astype(BF16)
    for q in range(S5_SLABS):
        t = jnp.dot(ub[:, q * S5_SLAB_IN:(q + 1) * S5_SLAB_IN], win_ref[q], preferred_element_type=F32)
        bu_ref[:, q * S5_SLAB_CH:(q + 1) * S5_SLAB_CH] = t[:, 0:S5_SLAB_CH]
        bu_ref[:, S5_CH + q * S5_SLAB_CH:S5_CH + (q + 1) * S5_SLAB_CH] = t[:, S5_SLAB_CH:2 * S5_SLAB_CH]
    ab_re = abre_ref[...]
    ab_im = abim_ref[...]

    def step(t, carry):
        r0 = pl.multiple_of(t * bsz, bsz)
        hr = hr_ref[...]
        hi = hi_ref[...]
        nr = ab_re * hr - ab_im * hi + bu_ref[pl.ds(r0, bsz), 0:S5_CH]
        ni = ab_re * hi + ab_im * hr + bu_ref[pl.ds(r0, bsz), S5_CH:2 * S5_CH]
        hr_ref[...] = nr
        hi_ref[...] = ni
        bu_ref[pl.ds(r0, bsz), 0:S5_CH] = nr
        bu_ref[pl.ds(r0, bsz), S5_CH:2 * S5_CH] = ni
        return carry

    lax.fori_loop(0, tt, step, 0)
    ys = []
    for q in range(S5_SLABS):
        h_re = bu_ref[:, q * S5_SLAB_CH:(q + 1) * S5_SLAB_CH].astype(BF16)
        h_im = bu_ref[:, S5_CH + q * S5_SLAB_CH:S5_CH + (q + 1) * S5_SLAB_CH].astype(BF16)
        ys.append(jnp.dot(h_re, wcre_ref[q], preferred_element_type=F32)
                  - jnp.dot(h_im, wcim_ref[q], preferred_element_type=F32))
    y = jnp.concatenate(ys, axis=1)
    y = jax.nn.gelu(y + d_ref[...] * u)
    y_ref[...] = y * _sigmoid(jnp.dot(y.astype(BF16), gw_ref[...], preferred_element_type=F32) + gb_ref[...])


def _s5(u, bsz, tt, w_in_bd, w_cre_bd, w_cim_bd, ab_re, ab_im, d_skip, glu_w, glu_b, hr0, hi0):
    rows = u.shape[0]
    tb = tt * bsz
    assert rows % tb == 0
    row = pl.BlockSpec((tb, S5_WIDTH), lambda i: (i, 0))
    st = _const_spec((bsz, S5_CH))
    st_in = _const_in((bsz, S5_CH))
    return pl.pallas_call(
        functools.partial(_s5_kernel, bsz=bsz, tt=tt),
        grid=(rows // tb,),
        in_specs=[row, _const_in((S5_SLABS, S5_SLAB_IN, 2 * S5_SLAB_CH)), _const_in((S5_SLABS, S5_SLAB_CH, S5_SLAB_IN)),
                  _const_in((S5_SLABS, S5_SLAB_CH, S5_SLAB_IN)), _const_in((1, S5_CH)), _const_in((1, S5_CH)),
                  _const_in((1, S5_WIDTH)), _const_in((S5_WIDTH, S5_WIDTH)), _const_in((1, S5_WIDTH)), st_in, st_in],
        out_specs=(row, st, st),
        out_shape=(jax.ShapeDtypeStruct((rows, S5_WIDTH), F32), jax.ShapeDtypeStruct((bsz, S5_CH), F32),
                   jax.ShapeDtypeStruct((bsz, S5_CH), F32)),
        scratch_shapes=[pltpu.VMEM((tb, 2 * S5_CH), F32)],
        compiler_params=_params(("arbitrary",)), name="s5_mixer",
    )(u, w_in_bd, w_cre_bd, w_cim_bd, ab_re, ab_im, d_skip, glu_w, glu_b, hr0, hi0)


def _hg_kernel(x_ref, s0_ref, lb_ref, nw_ref, y_ref, s_ref, x_dense, *, lseg, nb, sub):
    @pl.when(pl.program_id(1) == 0)
    def _():
        s_ref[...] = s0_ref[...]

    rows = lseg * nb
    nlev = int(math.log2(lseg))
    lb = lb_ref[...]
    i, j = _iotas(rows)
    hd = lambda t, h: t[:, h * HG_HEADDIM:(h + 1) * HG_HEADDIM]
    seg = lambda t, s: t[s * lseg:(s + 1) * lseg]
    cat = lambda parts: parts[0] if nb == 1 else jnp.concatenate(parts, axis=0)

    SMALL = 8
    lower, upper, pair = {}, {}, []
    for l in range(nlev + 1):
        bi = i >> l
        bj = j >> l
        if (1 << l) < SMALL or l == nlev:
            lower[l] = ((bi == bj) & (j <= i)).astype(BF16)
            upper[l] = ((bi == bj) & (j > i)).astype(BF16)
        pair.append((bi == bj + 1) & ((bi & 1) == 1))
    diag_mask = i == j
    col_seg = [(j[0:1, :] >> nlev) == s for s in range(nb)]
    rows_of = lambda t, h: t[h * HG_HEADDIM:(h + 1) * HG_HEADDIM]

    def block_row(t, size, which):
        t3 = t.reshape(rows // size, size, t.shape[1])
        return jnp.broadcast_to(t3[:, which:which + 1, :], t3.shape).reshape(t.shape)

    def problem(p):
        x = _load_rows(x_ref, p, nb, x_dense)
        states = [[s_ref[p * nb + s, h] for s in range(nb)] for h in range(HG_HEADS)]
        q = _silu(x[:, 0:HG_WIDTH])
        f = lb + (1.0 - lb) * _sigmoid(x[:, HG_WIDTH:2 * HG_WIDTH])
        logf = jnp.log(f)
        k = 1.0 - f
        v = x[:, 2 * HG_WIDTH:3 * HG_WIDTH].astype(BF16)
        gate = _sigmoid(x[:, 3 * HG_WIDTH:4 * HG_WIDTH])
        logf_parts = _split(logf, 3)
        msum = lambda m, n: sum(jnp.dot(m, part, preferred_element_type=F32) for part in logf_parts[:n])
        cum = msum(lower[nlev], 3)
        cum_ex = cum - logf

        att = [jnp.where(diag_mask, jnp.sum(hd(q, h) * hd(k, h), axis=-1, keepdims=True), 0.0)
               for h in range(HG_HEADS)]
        for l in range(nlev):
            size = 1 << l
            if l == 0:
                qe, ke = q * f, k
            elif size < SMALL:
                qe, ke = q * jnp.exp(msum(lower[l], 2)), k * jnp.exp(msum(upper[l], 2))
            else:
                qe = q * jnp.exp(cum - block_row(cum_ex, size, 0))
                ke = k * jnp.exp(block_row(cum, size, size - 1) - cum)
            qe = qe.astype(BF16)
            ke_t = ke.T.astype(BF16)
            for h in range(HG_HEADS):
                att[h] = att[h] + jnp.where(pair[l], _bdot(hd(qe, h), rows_of(ke_t, h)), 0.0)

        qin = (q * jnp.exp(cum)).astype(BF16)
        seg_tot = block_row(cum, lseg, lseg - 1)
        kend_t = (k * jnp.exp(seg_tot - cum)).T.astype(BF16)
        seg_tot_t = seg_tot.T
        outs, new_states = [], []
        for h in range(HG_HEADS):
            vh = hd(v, h)
            o = _bdot(att[h], vh) + cat([_bdot(seg(hd(qin, h), s), states[h][s]) for s in range(nb)])
            ns = []
            for s in range(nb):
                tot = jnp.broadcast_to(rows_of(seg_tot_t, h)[:, s * lseg:s * lseg + 1], (HG_HEADDIM, HG_HEADDIM))
                kend_s = rows_of(kend_t, h) if nb == 1 else jnp.where(col_seg[s], rows_of(kend_t, h), 0)
                ns.append(jnp.exp(tot) * states[h][s] + _bdot(kend_s, vh))
            new_states.append(ns)
            outs.append(_rms(o, nw_ref[...]) * hd(gate, h))
        _store_rows(y_ref, p, jnp.concatenate(outs, axis=1), nb, lseg)
        for h in range(HG_HEADS):
            for s in range(nb):
                s_ref[p * nb + s, h] = new_states[h][s]

    _for_each_problem(sub, nb, problem)


def _hgrn(x, s0, lb, norm_w, stacked, layer, bsz, seqlen, lseg, nb, sub):
    return _mixer_call(_hg_kernel, "hgrn_mixer", (x,), s0, (lb, norm_w),
                       (_const_in((1, HG_WIDTH)), _const_in((1, HG_HEADDIM))), stacked, layer,
                       HG_WIDTH, (4 * HG_WIDTH,), bsz, seqlen, lseg, nb, sub)


def _rw_kernel(x_ref, s0_ref, w0_ref, wup_ref, a0_ref, aup_ref, gup_ref, kk_ref, ka_ref, rk_ref, lnw_ref, lnb_ref,
               y_ref, s_ref, x_dense, *, lseg, nb, sub):
    _init_states_transposed(s_ref, s0_ref, RW_HEADS, sub)

    R = lseg * nb
    HD, PW, NH, W = RW_HEADDIM, 2 * RW_HEADDIM, RW_HEADS, RW_WIDTH
    assert R == PW
    nlev = int(math.log2(lseg))
    i, j = _iotas(R)
    same = (i >> nlev) == (j >> nlev)
    incl = same & (j <= i)
    strict = same & (j < i)
    m_incl = incl.astype(BF16)
    m_same = same.astype(BF16)
    eye = (i == j).astype(F32)
    pair = [((i >> l) == (j >> l) + 1) & (((i >> l) & 1) == 1) for l in range(nlev)]
    m_head = ((i >> 6) == (j >> 6)).astype(BF16)
    col_seg = [(j[0:1, :] >> nlev) == s for s in range(nb)]
    seg_cols = lambda t, s: t if nb == 1 else jnp.where(col_seg[s], t, 0)
    first_col = jnp.concatenate([(i == s * lseg).astype(BF16) for s in range(nb)], axis=1)
    hd = lambda t, h: t[:, h * HD:(h + 1) * HD]
    rows_of = lambda t, h: t[h * HD:(h + 1) * HD]
    seg = lambda t, s: t[s * lseg:(s + 1) * lseg]
    cat = lambda parts: parts[0] if nb == 1 else jnp.concatenate(parts, axis=0)

    def head_sum(t):
        return jnp.concatenate([_dot_mask(t[:, q * PW:(q + 1) * PW], m_head, 2) for q in range(W // PW)], axis=1)

    def problem(p):
        x = _load_rows(x_ref, p, nb, x_dense)
        states = [[s_ref[p * nb + s, h] for s in range(nb)] for h in range(NH)]
        r, k, v = x[:, 0:W], x[:, W:2 * W], x[:, 2 * W:3 * W]
        wd, ad, gd = x[:, 3 * W:3 * W + 64], x[:, 3 * W + 64:3 * W + 128], x[:, 3 * W + 128:3 * W + 256]
        w = -jax.nn.softplus(-(w0_ref[...] + _bdot(jnp.tanh(wd), wup_ref[...]))) - 0.5
        logw = -jnp.exp(w)
        ag = _sigmoid(a0_ref[...] + _bdot(ad, aup_ref[...]))
        g = _bdot(_sigmoid(gd), gup_ref[...])
        logw_parts = _split(logw, 2)
        rsum = lambda m: (jnp.dot(m, logw_parts[0], preferred_element_type=F32)
                          + jnp.dot(m, logw_parts[1], preferred_element_type=F32))
        b_in = rsum(m_incl)
        b_tot = rsum(m_same)
        e_in = jnp.exp(b_in)
        e_ex = jnp.exp(b_in - logw)

        kk = k * kk_ref[...]
        kk = kk * lax.rsqrt(jnp.maximum(head_sum(kk * kk), 1e-24))
        k2 = k * (1.0 + (ag - 1.0) * ka_ref[...])
        kb = kk * ag
        a_t = (-kk * e_ex).astype(BF16)
        r_t = (r * e_in).astype(BF16)
        vb = v.astype(BF16)
        bonus = head_sum(r * k2 * rk_ref[...]) * v

        b_in_t = b_in.T
        b_tot_t = b_tot.T
        e_neg_t = jnp.exp(-b_in_t)
        e_end_t = jnp.exp(b_tot_t - b_in_t)
        kb_t = kb.T
        k2_t = k2.T
        b_n = (kb_t * e_neg_t).astype(BF16)
        k_n = (k2_t * e_neg_t).astype(BF16)
        b_e = (kb_t * e_end_t).astype(BF16)
        k_e = (k2_t * e_end_t).astype(BF16)
        tot_parts = _split(b_tot_t, 2)

        keys = [jnp.concatenate([rows_of(b_n, h), rows_of(k_n, h)], axis=1) for h in range(NH)]
        am = [_bdot(hd(a_t, h), keys[h]) for h in range(NH)]
        rm = [_bdot(hd(r_t, h), keys[h]) for h in range(NH)]
        a_ab = [jnp.where(strict, am[h][:, :R], 0.0) for h in range(NH)]
        a_ak = [jnp.where(strict, am[h][:, R:], 0.0) for h in range(NH)]
        a_rb = [jnp.where(incl, rm[h][:, :R], 0.0) for h in range(NH)]
        a_rk = [jnp.where(incl, rm[h][:, R:], 0.0) for h in range(NH)]
        zmat = [cat([_bdot(seg(hd(a_t, h), s), states[h][s]) for s in range(nb)]) + _bdot(a_ak[h], hd(vb, h))
                for h in range(NH)]
        y0 = [cat([_bdot(seg(hd(r_t, h), s), states[h][s]) for s in range(nb)]) + _bdot(a_rk[h], hd(vb, h))
              for h in range(NH)]
        inv = [eye + jnp.where(pair[0], a_ab[h], 0.0) for h in range(NH)]
        for l in range(1, nlev):
            tmp = [_bdot(jnp.where(pair[l], a_ab[h], 0.0), inv[h]) for h in range(NH)]
            inv = [inv[h] + _bdot(inv[h], tmp[h]) for h in range(NH)]
        sa = [_bdot(inv[h], zmat[h]) for h in range(NH)]
        ys = [y0[h] + _bdot(a_rb[h], sa[h]) for h in range(NH)]
        e_tot = [jnp.exp(jnp.dot(rows_of(tot_parts[0], h), first_col, preferred_element_type=F32)
                         + jnp.dot(rows_of(tot_parts[1], h), first_col, preferred_element_type=F32))
                 for h in range(NH)]
        new_states = [[states[h][s] * e_tot[h][:, s * PW:s * PW + HD]
                       + _bdot(seg_cols(rows_of(b_e, h), s), sa[h]) + _bdot(seg_cols(rows_of(k_e, h), s), hd(vb, h))
                       for s in range(nb)] for h in range(NH)]
        y = jnp.concatenate(ys, axis=1)
        mu = head_sum(y) * (1.0 / RW_HEADDIM)
        yc = y - mu
        var = head_sum(yc * yc) * (1.0 / RW_HEADDIM)
        y = yc * lax.rsqrt(var + RW_LN_EPS) * lnw_ref[...] + lnb_ref[...] + bonus
        _store_rows(y_ref, p, y * g, nb, lseg)
        for h in range(NH):
            for s in range(nb):
                s_ref[p * nb + s, h] = new_states[h][s]

    _for_each_problem(sub, nb, problem)
    _finish_states_transposed(s_ref, RW_HEADS, sub)


def _rwkv(x, s0, p, stacked, layer, bsz, seqlen, lseg, nb, sub):
    vec = _const_in((1, RW_WIDTH))
    return _mixer_call(_rw_kernel, "rwkv_mixer", (x,), s0,
                       (p['w0'], p['w_up'], p['a0'], p['a_up'], p['g_up'], p['k_k'], p['k_a'], p['r_k'], p['ln_w'],
                        p['ln_b']),
                       (vec, _const_in((64, RW_WIDTH)), vec, _const_in((64, RW_WIDTH)), _const_in((128, RW_WIDTH)),
                        vec, vec, vec, vec, vec), stacked, layer,
                       RW_WIDTH, (RW_PROJ,), bsz, seqlen, lseg, nb, sub)


def _merge_kernel(x_ref, yssd_ref, ys5_ref, yhg_ref, yrw_ref, nw_ref, wm_ref, bm_ref,
                  wssd_ref, ws5_ref, whg_ref, wrw_ref, wout_ref, o_ref):
    x = x_ref[...]
    xn = _rms(x, nw_ref[...]).astype(BF16)
    merged = None
    for b, (y_ref, w_ref) in enumerate(((yssd_ref, wssd_ref), (ys5_ref, ws5_ref), (yhg_ref, whg_ref), (yrw_ref, wrw_ref))):
        cs = slice(b * D_MODEL, (b + 1) * D_MODEL)
        gate = _sigmoid(jnp.dot(xn, wm_ref[:, cs], preferred_element_type=F32) + bm_ref[:, cs])
        t = gate * jnp.dot(y_ref[...].astype(BF16), w_ref[...], preferred_element_type=F32)
        merged = t if merged is None else merged + t
    o_ref[...] = x + jnp.dot(merged.astype(BF16), wout_ref[...], preferred_element_type=F32)


def _merge(x, y_ssd, y_s5, y_hg, y_rw, tb, norm_w, w_merge, b_merge, w_ssd, w_s5, w_hg, w_rw, w_out):
    rows = x.shape[0]
    row = lambda w: pl.BlockSpec((tb, w), lambda i: (i, 0))
    return pl.pallas_call(
        _merge_kernel, grid=(rows // tb,),
        in_specs=[row(D_MODEL), row(1024), row(S5_WIDTH), row(HG_WIDTH), row(RW_WIDTH),
                  _const_in((1, D_MODEL)), _const_in((D_MODEL, 4 * D_MODEL)), _const_in((1, 4 * D_MODEL)),
                  _const_in((1024, D_MODEL)), _const_in((S5_WIDTH, D_MODEL)), _const_in((HG_WIDTH, D_MODEL)),
                  _const_in((RW_WIDTH, D_MODEL)), _const_in((D_MODEL, D_MODEL))],
        out_specs=row(D_MODEL), out_shape=jax.ShapeDtypeStruct((rows, D_MODEL), F32),
        compiler_params=_params(("arbitrary",)), name="merge",
    )(x, y_ssd, y_s5, y_hg, y_rw, norm_w, w_merge, b_merge, w_ssd, w_s5, w_hg, w_rw, w_out)


def _ffn_kernel(x_ref, nw_ref, wup_ref, cw_ref, cb_ref, wdn_ref, conv0_ref, fnw_ref, o_ref, convst_ref, full_ref,
                *, bsz, tb, final_norm):
    lookback = (FFN_CONV - 1) * bsz

    @pl.when(pl.program_id(0) == 0)
    def _():
        full_ref[0:lookback, :] = conv0_ref[...]

    x = x_ref[...]
    xn = _rms(x, nw_ref[...]).astype(BF16)
    full_ref[lookback:lookback + tb, :] = jnp.dot(xn, wup_ref[...], preferred_element_type=F32)
    acc = cb_ref[...] + full_ref[0:tb, :] * cw_ref[0:1, :]
    for j in range(1, FFN_CONV):
        acc = acc + full_ref[j * bsz:j * bsz + tb, :] * cw_ref[j:j + 1, :]
    _shift_rows_down(full_ref, tb, lookback)
    convst_ref[...] = full_ref[0:lookback, :]
    hidden = jax.nn.gelu(acc[:, 0:D_FF]) * acc[:, D_FF:2 * D_FF]
    out = x + jnp.dot(hidden.astype(BF16), wdn_ref[...], preferred_element_type=F32)
    if final_norm:
        out = _rms(out, fnw_ref[...])
    o_ref[...] = out


def _ffn(x, bsz, tb, norm_w, w_up, conv_w, conv_b, w_down, conv0, final_w, final_norm):
    rows = x.shape[0]
    lookback = (FFN_CONV - 1) * bsz
    assert rows % tb == 0 and tb % bsz == 0
    row = pl.BlockSpec((tb, D_MODEL), lambda i: (i, 0))
    return pl.pallas_call(
        functools.partial(_ffn_kernel, bsz=bsz, tb=tb, final_norm=final_norm),
        grid=(rows // tb,),
        in_specs=[row, _const_in((1, D_MODEL)), _const_in((D_MODEL, 2 * D_FF)), _const_in((FFN_CONV, 2 * D_FF)),
                  _const_in((1, 2 * D_FF)), _const_in((D_FF, D_MODEL)), _const_in((lookback, 2 * D_FF)),
                  _const_in((1, D_MODEL))],
        out_specs=(row, _const_spec((lookback, 2 * D_FF))),
        out_shape=(jax.ShapeDtypeStruct((rows, D_MODEL), F32), jax.ShapeDtypeStruct((lookback, 2 * D_FF), F32)),
        scratch_shapes=[pltpu.VMEM((lookback + tb, 2 * D_FF), F32)],
        compiler_params=_params(("arbitrary",)), name="conv_ffn",
    )(x, norm_w, w_up, conv_w, conv_b, w_down, conv0, final_w)


def _pad_lanes(v, width=DT_PAD):
    return jnp.pad(v, (0, width - v.shape[0])).reshape(1, width)


def _layer_params(l, P, lb_all):
    p = {n: a[l] for n, a in P.items()}
    row = lambda a: a.reshape(1, -1)
    ab_re, ab_im, s5_win, s5_wcre, s5_wcim = _s5_prep(p['s5_log_dt'], p['s5_a_re'], p['s5_a_im'], p['s5_b_re'],
                                                      p['s5_b_im'], p['s5_c_re'], p['s5_c_im'])
    q = dict(
        norm1_w=row(p['norm1_w']), w_in=_pack_w_in(p['w_in']),
        ssd_conv_w=p['ssd_conv_w'], ssd_conv_b=row(p['ssd_conv_b']), ssd_dt_bias=_pad_lanes(p['ssd_dt_bias']),
        ssd_a_log=_pad_lanes(p['ssd_a_log']), ssd_d=_pad_lanes(p['ssd_d']), ssd_norm_w=row(p['ssd_norm_w']),
        s5_win=s5_win, s5_wcre=s5_wcre, s5_wcim=s5_wcim,
        s5_ab_re=row(ab_re), s5_ab_im=row(ab_im), s5_d=row(p['s5_d']),
        s5_glu_w=p['s5_glu_w'].astype(BF16), s5_glu_b=row(p['s5_glu_b']),
        hg_lb=lb_all[l:l + 1], hg_norm_w=row(p['hg_norm_w']),
        rw_mu=row(p['rw_mu']),
        rw=dict(w0=row(p['rw_w0']), w_up=p['rw_w_up'].astype(BF16), a0=row(p['rw_a0']), a_up=p['rw_a_up'].astype(BF16),
                g_up=p['rw_g_up'].astype(BF16), k_k=row(p['rw_k_k']), k_a=row(p['rw_k_a']), r_k=row(p['rw_r_k']),
                ln_w=row(p['rw_ln_w']), ln_b=row(p['rw_ln_b'])),
        w_merge=p['w_merge'].astype(BF16), b_merge=row(p['b_merge']),
        w_br_ssd=p['w_br_ssd'].astype(BF16), w_br_s5=p['w_br_s5'].astype(BF16), w_br_hg=p['w_br_hg'].astype(BF16),
        w_br_rw=p['w_br_rw'].astype(BF16), w_out=p['w_out'].astype(BF16),
        norm2_w=row(p['norm2_w']), ffn_up=p['ffn_up'].astype(BF16), ffn_conv_w=p['ffn_conv_w'],
        ffn_conv_b=row(p['ffn_conv_b']), ffn_down=p['ffn_down'].astype(BF16),
    )
    return q


class _Group:
    def __init__(self, bsz, seqlen):
        self.bsz, self.seqlen = bsz, seqlen
        rows = bsz * seqlen
        self.tb = min(rows, max(256, bsz))
        self.tt = self.tb // bsz
        if seqlen >= 128:
            self.ssd, self.hg, self.rw = (128, 2, 8), (64, 2, 8), (64, 2, 8)
        else:
            self.ssd = self.hg = self.rw = (seqlen, 128 // seqlen, 128 // seqlen)


def _time_major(a):
    a = jnp.swapaxes(a, 0, 1)
    return a.reshape((a.shape[0] * a.shape[1],) + a.shape[2:])


def _batch_major(a, bsz):
    a = a.reshape((a.shape[0] // bsz, bsz) + a.shape[1:])
    return jnp.swapaxes(a, 0, 1)


def _trunk(x, states, layers, final_norm_w, grp):
    bsz, seqlen = grp.bsz, grp.seqlen
    xt = _time_major(x)
    small_states = []
    ssd_all = hg_all = rw_all = None
    for l, q in enumerate(layers):
        s_ssd, s_conv, s_s5r, s_s5i, s_hg, s_rw, s_shift, s_fconv = states[l]
        z, xbc, dt, u, hg, rwx, conv_new, shift_new = _in_proj(
            xt, bsz, grp.tb, q['norm1_w'], q['w_in'], q['ssd_conv_w'], q['ssd_conv_b'], q['ssd_dt_bias'], q['rw_mu'],
            _time_major(s_conv), s_shift)
        y_ssd, ssd_all = _ssd(z, xbc, dt, s_ssd, q['ssd_a_log'], q['ssd_d'], q['ssd_norm_w'], ssd_all, l,
                              bsz, seqlen, *grp.ssd)
        y_s5, hr_new, hi_new = _s5(u, bsz, grp.tt, q['s5_win'], q['s5_wcre'], q['s5_wcim'], q['s5_ab_re'], q['s5_ab_im'],
                                   q['s5_d'], q['s5_glu_w'], q['s5_glu_b'],
                                   s_s5r.reshape(bsz, S5_CH), s_s5i.reshape(bsz, S5_CH))
        y_hg, hg_all = _hgrn(hg, s_hg, q['hg_lb'], q['hg_norm_w'], hg_all, l, bsz, seqlen, *grp.hg)
        y_rw, rw_all = _rwkv(rwx, s_rw, q['rw'], rw_all, l, bsz, seqlen, *grp.rw)
        x1 = _merge(xt, y_ssd, y_s5, y_hg, y_rw, grp.tb, q['norm1_w'], q['w_merge'], q['b_merge'],
                    q['w_br_ssd'], q['w_br_s5'], q['w_br_hg'], q['w_br_rw'], q['w_out'])
        xt, fconv_new = _ffn(x1, bsz, grp.tb, q['norm2_w'], q['ffn_up'], q['ffn_conv_w'], q['ffn_conv_b'],
                             q['ffn_down'], _time_major(s_fconv), final_norm_w.reshape(1, D_MODEL), l == DEPTH - 1)
        small_states.append((_batch_major(conv_new, bsz),
                             hr_new.reshape(bsz, S5_GROUPS, S5_STATE), hi_new.reshape(bsz, S5_GROUPS, S5_STATE),
                             shift_new, _batch_major(fconv_new, bsz)))
    conv_all, s5r_all, s5i_all, shift_all, fconv_all = (
        jnp.stack([st[k] for st in small_states], axis=0) for k in range(5))
    return _batch_major(xt, bsz), (ssd_all, conv_all, s5r_all, s5i_all, hg_all, rw_all, shift_all, fconv_all)


def _zero_states(bsz):
    z = lambda *s: jnp.zeros((bsz,) + s, F32)
    return (z(SSD_HEADS, SSD_HEADDIM, SSD_D_STATE), z(SSD_CONV - 1, SSD_CONV_CH), z(S5_GROUPS, S5_STATE),
            z(S5_GROUPS, S5_STATE), z(HG_HEADS, HG_HEADDIM, HG_HEADDIM), z(RW_HEADS, RW_HEADDIM, RW_HEADDIM),
            z(RW_PROJ), z(FFN_CONV - 1, 2 * D_FF))


def kernel(x_prompt, x_sample, state_ssd, state_ssd_conv, state_s5_re, state_s5_im, state_hgrn, state_rwkv, state_rwkv_shift, state_ffn_conv, norm1_w, w_in, ssd_conv_w, ssd_conv_b, ssd_dt_bias, ssd_a_log, ssd_d, ssd_norm_w, s5_a_re, s5_a_im, s5_log_dt, s5_b_re, s5_b_im, s5_c_re, s5_c_im, s5_d, s5_glu_w, s5_glu_b, hg_lb_raw, hg_norm_w, rw_mu, rw_w0, rw_w_up, rw_a0, rw_a_up, rw_g_up, rw_k_k, rw_k_a, rw_r_k, rw_ln_w, rw_ln_b, w_br_ssd, w_br_s5, w_br_hg, w_br_rw, w_merge, b_merge, w_out, norm2_w, ffn_up, ffn_conv_w, ffn_conv_b, ffn_down, final_norm_w):
    P = dict(norm1_w=norm1_w, w_in=w_in, ssd_conv_w=ssd_conv_w, ssd_conv_b=ssd_conv_b, ssd_dt_bias=ssd_dt_bias,
             ssd_a_log=ssd_a_log, ssd_d=ssd_d, ssd_norm_w=ssd_norm_w, s5_a_re=s5_a_re, s5_a_im=s5_a_im,
             s5_log_dt=s5_log_dt, s5_b_re=s5_b_re, s5_b_im=s5_b_im, s5_c_re=s5_c_re, s5_c_im=s5_c_im, s5_d=s5_d,
             s5_glu_w=s5_glu_w, s5_glu_b=s5_glu_b, hg_norm_w=hg_norm_w, rw_mu=rw_mu, rw_w0=rw_w0, rw_w_up=rw_w_up,
             rw_a0=rw_a0, rw_a_up=rw_a_up, rw_g_up=rw_g_up,
             rw_k_k=rw_k_k.reshape(DEPTH, RW_WIDTH), rw_k_a=rw_k_a.reshape(DEPTH, RW_WIDTH),
             rw_r_k=rw_r_k.reshape(DEPTH, RW_WIDTH), rw_ln_w=rw_ln_w.reshape(DEPTH, RW_WIDTH),
             rw_ln_b=rw_ln_b.reshape(DEPTH, RW_WIDTH),
             w_br_ssd=w_br_ssd, w_br_s5=w_br_s5, w_br_hg=w_br_hg, w_br_rw=w_br_rw, w_merge=w_merge, b_merge=b_merge,
             w_out=w_out, norm2_w=norm2_w, ffn_up=ffn_up, ffn_conv_w=ffn_conv_w, ffn_conv_b=ffn_conv_b,
             ffn_down=ffn_down)
    lb_all = _lower_bounds(hg_lb_raw)
    layers = [_layer_params(l, P, lb_all) for l in range(DEPTH)]
    sample_states = (state_ssd, state_ssd_conv, state_s5_re, state_s5_im, state_hgrn, state_rwkv,
                     state_rwkv_shift, state_ffn_conv)
    sample_init = [tuple(s[l] for s in sample_states) for l in range(DEPTH)]
    prompt_init = [_zero_states(x_prompt.shape[0])] * DEPTH
    y_prompt, p_states = _trunk(x_prompt, prompt_init, layers, final_norm_w, _Group(*x_prompt.shape[:2]))
    y_sample, s_states = _trunk(x_sample, sample_init, layers, final_norm_w, _Group(*x_sample.shape[:2]))
    return (y_prompt, y_sample) + p_states + s_states
```

```python
import functools
import math

import jax
import jax.numpy as jnp
from jax import lax
from jax.experimental import pallas as pl
from jax.experimental.pallas import tpu as pltpu

F32 = jnp.float32
BF16 = jnp.bfloat16

D_MODEL = 1024
DEPTH = 4
SSD_HEADS = 16
SSD_HEADDIM = 64
SSD_D_STATE = 64
SSD_GROUPS = 4
SSD_CONV = 4
SSD_CONV_CH = 1536
S5_WIDTH = 512
S5_GROUPS = 32
S5_GROUP = 16
S5_STATE = 64
S5_CH = S5_GROUPS * S5_STATE
HG_WIDTH = 512
HG_HEADS = 4
HG_HEADDIM = 128
RW_WIDTH = 512
RW_HEADS = 8
RW_HEADDIM = 64
RW_PROJ = 1792
RW_LN_EPS = 64e-5
D_FF = 2816
FFN_CONV = 3
EPS = 1e-6
IN_SIZES = (1024, 1536, 16, 512, 512, 512, 512, 512, 1792)
DT_PAD = 128

VMEM_LIMIT = 56 * 1024 * 1024


def _bdot(a, b):
    return jnp.dot(a.astype(BF16), b.astype(BF16), preferred_element_type=F32)


def _split(x, n):
    parts = []
    r = x
    for _ in range(n):
        p = r.astype(BF16)
        parts.append(p)
        r = r - p.astype(F32)
    return parts


def _mask_dot(m, x, n=3):
    out = None
    for p in _split(x, n):
        t = jnp.dot(m, p, preferred_element_type=F32)
        out = t if out is None else out + t
    return out


def _dot_mask(x, m, n=3):
    out = None
    for p in _split(x, n):
        t = jnp.dot(p, m, preferred_element_type=F32)
        out = t if out is None else out + t
    return out


def _rms(x, w):
    return x * lax.rsqrt(jnp.mean(x * x, axis=-1, keepdims=True) + EPS) * w


def _sigmoid(x):
    return jax.nn.sigmoid(x)


def _silu(x):
    return x * jax.nn.sigmoid(x)


def _load_rows(ref, p, nb, dense_ref):
    lseg = ref.shape[0]
    for s in range(nb):
        dense_ref[s * lseg:(s + 1) * lseg, :] = ref[:, p * nb + s, :]
    return dense_ref[...]


def _store_rows(ref, p, y, nb, lseg):
    for s in range(nb):
        ref[:, p * nb + s, :] = y[s * lseg:(s + 1) * lseg, :]


def _init_states_transposed(s_ref, s0_ref, nheads, sub):
    @pl.when(pl.program_id(1) == 0)
    def _():
        def body(b, carry):
            for h in range(nheads):
                s_ref[b, h] = s0_ref[b, h].T
            return carry
        lax.fori_loop(0, sub, body, 0)


def _finish_states_transposed(s_ref, nheads, sub):
    @pl.when(pl.program_id(1) == pl.num_programs(1) - 1)
    def _():
        def body(b, carry):
            for h in range(nheads):
                s_ref[b, h] = s_ref[b, h].T
            return carry
        lax.fori_loop(0, sub, body, 0)


def _for_each_problem(sub, nb, problem):
    if nb == sub:
        problem(0)
    else:
        def body(p, carry):
            problem(p)
            return carry
        lax.fori_loop(0, sub // nb, body, 0)


def _shift_rows_down(ref, dist, n):
    for off in range(0, n, dist):
        m = min(dist, n - off)
        ref[off:off + m, :] = ref[off + dist:off + dist + m, :]


def _iotas(rows):
    i = lax.broadcasted_iota(jnp.int32, (rows, rows), 0)
    j = lax.broadcasted_iota(jnp.int32, (rows, rows), 1)
    return i, j


def _const_spec(shape):
    nd = len(shape)
    return pl.BlockSpec(shape, lambda *_: (0,) * nd)


def _const_in(shape):
    nd = len(shape)
    return pl.BlockSpec(shape, lambda *_: (0,) * nd, pipeline_mode=pl.Buffered(1))


def _params(sem):
    return pltpu.CompilerParams(dimension_semantics=sem, vmem_limit_bytes=VMEM_LIMIT)


def _lb_kernel(raw_ref, o_ref):
    raw = raw_ref[...]
    m = jnp.max(raw, axis=0, keepdims=True)
    e = jnp.exp(raw - m)
    sm = e / jnp.sum(e, axis=0, keepdims=True)
    acc = jnp.zeros_like(sm[0:1])
    rows = []
    for l in range(DEPTH):
        acc = acc + sm[l:l + 1]
        rows.append(acc)
    first = rows[0]
    o_ref[...] = jnp.concatenate([r - first for r in rows], axis=0)


def _lower_bounds(hg_lb_raw):
    return pl.pallas_call(_lb_kernel, out_shape=jax.ShapeDtypeStruct((DEPTH, HG_WIDTH), F32),
                          name="hg_lower_bounds")(hg_lb_raw)


S5_PACK = 8
S5_SLABS = S5_GROUPS // S5_PACK
S5_SLAB_IN = S5_PACK * S5_GROUP
S5_SLAB_CH = S5_PACK * S5_STATE


def _s5_prep_kernel(ldt_ref, are_ref, aim_ref, bre_ref, bim_ref, cre_ref, cim_ref,
                    abre_ref, abim_ref, win_ref, wcre_ref, wcim_ref):
    dt = jnp.exp(ldt_ref[...])
    a_re = are_ref[...]
    a_im = aim_ref[...]
    mag = jnp.exp(dt * a_re)
    ab_re = mag * jnp.cos(dt * a_im)
    ab_im = mag * jnp.sin(dt * a_im)
    den = a_re * a_re + a_im * a_im
    q_re = ((ab_re - 1.0) * a_re + ab_im * a_im) / den
    q_im = (ab_im * a_re - (ab_re - 1.0) * a_im) / den
    abre_ref[...] = ab_re
    abim_ref[...] = ab_im
    win_ref[...] = jnp.zeros(win_ref.shape, win_ref.dtype)
    wcre_ref[...] = jnp.zeros(wcre_ref.shape, wcre_ref.dtype)
    wcim_ref[...] = jnp.zeros(wcim_ref.shape, wcim_ref.dtype)
    for g in range(S5_GROUPS):
        slab, r = divmod(g, S5_PACK)
        rows_in = slice(r * S5_GROUP, (r + 1) * S5_GROUP)
        ch = slice(r * S5_STATE, (r + 1) * S5_STATE)
        ch_im = slice(S5_SLAB_CH + r * S5_STATE, S5_SLAB_CH + (r + 1) * S5_STATE)
        b_re = bre_ref[g * S5_GROUP:(g + 1) * S5_GROUP, :]
        b_im = bim_ref[g * S5_GROUP:(g + 1) * S5_GROUP, :]
        win_ref[slab, rows_in, ch] = (q_re[g:g + 1] * b_re - q_im[g:g + 1] * b_im).astype(win_ref.dtype)
        win_ref[slab, rows_in, ch_im] = (q_re[g:g + 1] * b_im + q_im[g:g + 1] * b_re).astype(win_ref.dtype)
        wcre_ref[slab, ch, rows_in] = cre_ref[g * S5_STATE:(g + 1) * S5_STATE, :].astype(wcre_ref.dtype)
        wcim_ref[slab, ch, rows_in] = cim_ref[g * S5_STATE:(g + 1) * S5_STATE, :].astype(wcim_ref.dtype)


def _s5_prep(log_dt, a_re, a_im, b_re, b_im, c_re, c_im):
    bt_re = jnp.swapaxes(b_re, 1, 2).reshape(S5_WIDTH, S5_STATE)
    bt_im = jnp.swapaxes(b_im, 1, 2).reshape(S5_WIDTH, S5_STATE)
    ct_re = jnp.swapaxes(c_re, 1, 2).reshape(S5_CH, S5_GROUP)
    ct_im = jnp.swapaxes(c_im, 1, 2).reshape(S5_CH, S5_GROUP)
    gn = jax.ShapeDtypeStruct((S5_GROUPS, S5_STATE), F32)
    return pl.pallas_call(
        _s5_prep_kernel,
        out_shape=(gn, gn, jax.ShapeDtypeStruct((S5_SLABS, S5_SLAB_IN, 2 * S5_SLAB_CH), BF16),
                   jax.ShapeDtypeStruct((S5_SLABS, S5_SLAB_CH, S5_SLAB_IN), BF16),
                   jax.ShapeDtypeStruct((S5_SLABS, S5_SLAB_CH, S5_SLAB_IN), BF16)),
        name="s5_discretise")(log_dt.reshape(S5_GROUPS, 1), a_re, a_im, bt_re, bt_im, ct_re, ct_im)


SEG_Z = (0, 1024)
SEG_XBC = (1024, 1536)
SEG_U = (2560, 512)
SEG_HG = (3072, 2048)
SEG_RW = (5120, 1792)
SEG_DT = (6912, DT_PAD)
IN_PACKED = 7040


def _pack_w_in(w_in):
    off = [0]
    for s in IN_SIZES:
        off.append(off[-1] + s)
    z, xbc, dt, u, q, f, i, g, rw = (w_in[:, off[k]:off[k + 1]] for k in range(9))
    dt = jnp.pad(dt, ((0, 0), (0, DT_PAD - dt.shape[1])))
    return jnp.concatenate([z, xbc, u, q, f, i, g, rw, dt], axis=1).astype(BF16)


def _in_kernel(x_ref, nw_ref, w_ref, cw_ref, cb_ref, dtb_ref, mu_ref, conv0_ref, sh0_ref,
               z_ref, xbc_ref, dt_ref, u_ref, hg_ref, rw_ref, convst_ref, shst_ref,
               full_ref, rwfull_ref, *, bsz, tb):
    step = pl.program_id(0)
    lookback = (SSD_CONV - 1) * bsz

    @pl.when(step == 0)
    def _():
        full_ref[0:lookback, :] = conv0_ref[...]
        rwfull_ref[0:bsz, :] = sh0_ref[...]

    xn = _rms(x_ref[...], nw_ref[...]).astype(BF16)

    def proj(seg):
        return jnp.dot(xn, w_ref[:, seg[0]:seg[0] + seg[1]], preferred_element_type=F32)

    z_ref[...] = proj(SEG_Z)
    u_ref[...] = proj(SEG_U)
    hg_ref[...] = proj(SEG_HG)
    dt_ref[...] = jax.nn.softplus(proj(SEG_DT) + dtb_ref[...])

    full_ref[lookback:lookback + tb, :] = proj(SEG_XBC)
    acc = cb_ref[...] + full_ref[0:tb, :] * cw_ref[0:1, :]
    for j in range(1, SSD_CONV):
        acc = acc + full_ref[j * bsz:j * bsz + tb, :] * cw_ref[j:j + 1, :]
    xbc_ref[...] = _silu(acc)
    _shift_rows_down(full_ref, tb, lookback)
    convst_ref[...] = full_ref[0:lookback, :]

    rwfull_ref[bsz:bsz + tb, :] = proj(SEG_RW)
    cur = rwfull_ref[bsz:bsz + tb, :]
    prev = rwfull_ref[0:tb, :]
    rw_ref[...] = cur + (prev - cur) * mu_ref[...]
    last = rwfull_ref[tb:tb + bsz, :]
    rwfull_ref[0:bsz, :] = last
    shst_ref[...] = last


def _in_proj(x, bsz, tb, norm_w, w_packed, conv_w, conv_b, dt_bias, mu, conv0, shift0):
    rows = x.shape[0]
    lookback = (SSD_CONV - 1) * bsz
    assert rows % tb == 0 and tb % bsz == 0
    row = lambda w: pl.BlockSpec((tb, w), lambda i: (i, 0))
    widths = (1024, SSD_CONV_CH, DT_PAD, S5_WIDTH, 4 * HG_WIDTH, RW_PROJ)
    out_shape = tuple(jax.ShapeDtypeStruct((rows, w), F32) for w in widths) + (
        jax.ShapeDtypeStruct((lookback, SSD_CONV_CH), F32), jax.ShapeDtypeStruct((bsz, RW_PROJ), F32))
    out_specs = tuple(row(w) for w in widths) + (_const_spec((lookback, SSD_CONV_CH)), _const_spec((bsz, RW_PROJ)))
    in_specs = [row(D_MODEL), _const_in((1, D_MODEL)), _const_in((D_MODEL, IN_PACKED)),
                _const_in((SSD_CONV, SSD_CONV_CH)), _const_in((1, SSD_CONV_CH)), _const_in((1, DT_PAD)),
                _const_in((1, RW_PROJ)), _const_in((lookback, SSD_CONV_CH)), _const_in((bsz, RW_PROJ))]
    return pl.pallas_call(
        functools.partial(_in_kernel, bsz=bsz, tb=tb),
        grid=(rows // tb,), in_specs=in_specs, out_specs=out_specs, out_shape=out_shape,
        scratch_shapes=[pltpu.VMEM((lookback + tb, SSD_CONV_CH), F32), pltpu.VMEM((bsz + tb, RW_PROJ), F32)],
        compiler_params=_params(("arbitrary",)), name="in_proj",
    )(x, norm_w, w_packed, conv_w, conv_b, dt_bias, mu, conv0, shift0)


def _ssd_kernel(z_ref, xbc_ref, dt_ref, s0_ref, alog_ref, d_ref, nw_ref, y_ref, s_ref,
                z_dense, xbc_dense, dt_dense, *, lseg, nb, sub):
    _init_states_transposed(s_ref, s0_ref, SSD_HEADS, sub)

    rows = lseg * nb
    sh = int(math.log2(lseg))
    n_x = SSD_HEADS * SSD_HEADDIM
    gw = SSD_GROUPS * SSD_D_STATE
    hpg = SSD_HEADS // SSD_GROUPS
    neg_a = -jnp.exp(alog_ref[...])
    d_skip = d_ref[...]
    i, j = _iotas(rows)
    same = (i >> sh) == (j >> sh)
    tril = same & (j <= i)
    m_tril = tril.astype(BF16)
    m_triu = (same & (i <= j)).astype(BF16)
    m_same = same.astype(BF16)
    seg = lambda t, s: t[s * lseg:(s + 1) * lseg]
    cat = lambda parts: parts[0] if nb == 1 else jnp.concatenate(parts, axis=0)

    col_seg = [(j[0:1, :] >> sh) == s for s in range(nb)]
    head_lanes = (lax.broadcasted_iota(jnp.int32, (DT_PAD, n_x), 0)
                  == (lax.broadcasted_iota(jnp.int32, (DT_PAD, n_x), 1) >> 6)).astype(BF16)

    def problem(p):
        z = _load_rows(z_ref, p, nb, z_dense)
        xbc = _load_rows(xbc_ref, p, nb, xbc_dense)
        dt = _load_rows(dt_ref, p, nb, dt_dense)
        states = [[s_ref[p * nb + s, h] for s in range(nb)] for h in range(SSD_HEADS)]
        a = dt * neg_a
        cum = _mask_dot(m_tril, a, 2)
        cum_t = _dot_mask(a.T, m_triu, 2)
        tot = _mask_dot(m_same, a, 2)
        dec_end = jnp.exp(tot - cum)
        dec_in = jnp.exp(cum)
        dec_tot = jnp.exp(tot)
        bm_t = xbc[:, n_x:n_x + gw].T.astype(BF16)
        bts = [bm_t[g * SSD_D_STATE:(g + 1) * SSD_D_STATE] for g in range(SSD_GROUPS)]
        if nb > 1:
            zero = jnp.zeros_like(bts[0])
            bts_seg = [[jnp.where(col_seg[s], bts[g], zero) for s in range(nb)] for g in range(SSD_GROUPS)]
        else:
            bts_seg = [[bts[g]] for g in range(SSD_GROUPS)]
        cms = [xbc[:, n_x + gw + g * SSD_D_STATE:n_x + gw + (g + 1) * SSD_D_STATE].astype(BF16)
               for g in range(SSD_GROUPS)]
        gmats = [_bdot(cms[g], bts[g]) for g in range(SSD_GROUPS)]
        xs = xbc[:, 0:n_x]
        xdt_all = xs * _dot_mask(dt, head_lanes, 2)
        xdec_all = (xdt_all * _dot_mask(dec_end, head_lanes, 2)).astype(BF16)
        xdt_all = xdt_all.astype(BF16)
        ys, offs, new_states = [], [], []
        for h in range(SSD_HEADS):
            g = h // hpg
            hs = slice(h * SSD_HEADDIM, (h + 1) * SSD_HEADDIM)
            diff = cum[:, h:h + 1] - cum_t[h:h + 1, :]
            lmat = jnp.where(tril, jnp.exp(jnp.minimum(diff, 0.0)), 0.0)
            ys.append(_bdot(gmats[g] * lmat, xdt_all[:, hs]))
            offs.append(cat([_bdot(seg(cms[g], s), states[h][s]) for s in range(nb)]))
            new_states.append([states[h][s] * dec_tot[s * lseg:s * lseg + 1, h:h + 1]
                               + _bdot(bts_seg[g][s], xdec_all[:, hs]) for s in range(nb)])
        yall = (jnp.concatenate(ys, axis=1) + jnp.concatenate(offs, axis=1) * _dot_mask(dec_in, head_lanes, 2)
                + xs * _dot_mask(d_skip, head_lanes, 2))
        yall = _rms(yall * _silu(z), nw_ref[...])
        _store_rows(y_ref, p, yall, nb, lseg)
        for h in range(SSD_HEADS):
            for s in range(nb):
                s_ref[p * nb + s, h] = new_states[h][s]

    _for_each_problem(sub, nb, problem)
    _finish_states_transposed(s_ref, SSD_HEADS, sub)


def _without_ref(fn, idx):
    def body(*refs):
        return fn(*refs[:idx], *refs[idx + 1:])
    return body


def _mixer_call(kernel_fn, name, xs, s0, consts, const_specs, stacked, layer, width_out, scratch_widths,
                bsz, seqlen, lseg, nb, sub):
    blk = lambda w: pl.BlockSpec((lseg, sub, w), lambda b, c: (c, b, 0))
    zeros = (0,) * (s0.ndim - 1)
    st_in = pl.BlockSpec((sub,) + s0.shape[1:], lambda b, c: (b,) + zeros)
    st_out = pl.BlockSpec((None, sub) + s0.shape[1:], lambda b, c: (layer, b) + zeros)
    if stacked is None:
        stacked = jnp.zeros((DEPTH,) + s0.shape, F32)
    args = [x.reshape(seqlen, bsz, x.shape[1]) for x in xs] + [s0] + list(consts) + [stacked]
    in_specs = [blk(x.shape[1]) for x in xs] + [st_in] + list(const_specs) + [pl.BlockSpec(memory_space=pl.ANY)]
    stacked_idx = len(args) - 1
    body = _without_ref(functools.partial(kernel_fn, lseg=lseg, nb=nb, sub=sub), stacked_idx)
    y, s = pl.pallas_call(
        body, grid=(bsz // sub, seqlen // lseg), in_specs=in_specs, out_specs=(blk(width_out), st_out),
        out_shape=(jax.ShapeDtypeStruct((seqlen, bsz, width_out), F32),
                   jax.ShapeDtypeStruct((DEPTH,) + s0.shape, F32)),
        scratch_shapes=[pltpu.VMEM((nb * lseg, w), F32) for w in scratch_widths],
        input_output_aliases={stacked_idx: 1}, compiler_params=_params(("arbitrary", "arbitrary")), name=name,
    )(*args)
    return y.reshape(seqlen * bsz, width_out), s


def _ssd(z, xbc, dt, s0, a_log, d_skip, norm_w, stacked, layer, bsz, seqlen, lseg, nb, sub):
    n_x = SSD_HEADS * SSD_HEADDIM
    return _mixer_call(_ssd_kernel, "ssd_mixer", (z, xbc, dt), s0, (a_log, d_skip, norm_w),
                       (_const_in((1, DT_PAD)), _const_in((1, DT_PAD)), _const_in((1, n_x))), stacked, layer,
                       n_x, (n_x, SSD_CONV_CH, DT_PAD), bsz, seqlen, lseg, nb, sub)


def _s5_kernel(u_ref, win_ref, wcre_ref, wcim_ref, abre_ref, abim_ref, d_ref, gw_ref, gb_ref, hr0_ref, hi0_ref,
               y_ref, hr_ref, hi_ref, bu_ref, *, bsz, tt):
    @pl.when(pl.program_id(0) == 0)
    def _():
        hr_ref[...] = hr0_ref[...]
        hi_ref[...] = hi0_ref[...]

    u = u_ref[...]
    ub = u.astype(BF16)
    for q in range(S5_SLABS):
        t = jnp.dot(ub[:, q * S5_SLAB_IN:(q + 1) * S5_SLAB_IN], win_ref[q], preferred_element_type=F32)
        bu_ref[:, q * S5_SLAB_CH:(q + 1) * S5_SLAB_CH] = t[:, 0:S5_SLAB_CH]
        bu_ref[:, S5_CH + q * S5_SLAB_CH:S5_CH + (q + 1) * S5_SLAB_CH] = t[:, S5_SLAB_CH:2 * S5_SLAB_CH]
    ab_re = abre_ref[...]
    ab_im = abim_ref[...]

    def step(t, carry):
        r0 = pl.multiple_of(t * bsz, bsz)
        hr = hr_ref[...]
        hi = hi_ref[...]
        nr = ab_re * hr - ab_im * hi + bu_ref[pl.ds(r0, bsz), 0:S5_CH]
        ni = ab_re * hi + ab_im * hr + bu_ref[pl.ds(r0, bsz), S5_CH:2 * S5_CH]
        hr_ref[...] = nr
        hi_ref[...] = ni
        bu_ref[pl.ds(r0, bsz), 0:S5_CH] = nr
        bu_ref[pl.ds(r0, bsz), S5_CH:2 * S5_CH] = ni
        return carry

    lax.fori_loop(0, tt, step, 0)
    ys = []
    for q in range(S5_SLABS):
        h_re = bu_ref[:, q * S5_SLAB_CH:(q + 1) * S5_SLAB_CH].astype(BF16)
        h_im = bu_ref[:, S5_CH + q * S5_SLAB_CH:S5_CH + (q + 1) * S5_SLAB_CH].astype(BF16)
        ys.append(jnp.dot(h_re, wcre_ref[q], preferred_element_type=F32)
                  - jnp.dot(h_im, wcim_ref[q], preferred_element_type=F32))
    y = jnp.concatenate(ys, axis=1)
    y = jax.nn.gelu(y + d_ref[...] * u)
    y_ref[...] = y * _sigmoid(jnp.dot(y.astype(BF16), gw_ref[...], preferred_element_type=F32) + gb_ref[...])


def _s5(u, bsz, tt, w_in_bd, w_cre_bd, w_cim_bd, ab_re, ab_im, d_skip, glu_w, glu_b, hr0, hi0):
    rows = u.shape[0]
    tb = tt * bsz
    assert rows % tb == 0
    row = pl.BlockSpec((tb, S5_WIDTH), lambda i: (i, 0))
    st = _const_spec((bsz, S5_CH))
    st_in = _const_in((bsz, S5_CH))
    return pl.pallas_call(
        functools.partial(_s5_kernel, bsz=bsz, tt=tt),
        grid=(rows // tb,),
        in_specs=[row, _const_in((S5_SLABS, S5_SLAB_IN, 2 * S5_SLAB_CH)), _const_in((S5_SLABS, S5_SLAB_CH, S5_SLAB_IN)),
                  _const_in((S5_SLABS, S5_SLAB_CH, S5_SLAB_IN)), _const_in((1, S5_CH)), _const_in((1, S5_CH)),
                  _const_in((1, S5_WIDTH)), _const_in((S5_WIDTH, S5_WIDTH)), _const_in((1, S5_WIDTH)), st_in, st_in],
        out_specs=(row, st, st),
        out_shape=(jax.ShapeDtypeStruct((rows, S5_WIDTH), F32), jax.ShapeDtypeStruct((bsz, S5_CH), F32),
                   jax.ShapeDtypeStruct((bsz, S5_CH), F32)),
        scratch_shapes=[pltpu.VMEM((tb, 2 * S5_CH), F32)],
        compiler_params=_params(("arbitrary",)), name="s5_mixer",
    )(u, w_in_bd, w_cre_bd, w_cim_bd, ab_re, ab_im, d_skip, glu_w, glu_b, hr0, hi0)


def _hg_kernel(x_ref, s0_ref, lb_ref, nw_ref, y_ref, s_ref, x_dense, *, lseg, nb, sub):
    @pl.when(pl.program_id(1) == 0)
    def _():
        s_ref[...] = s0_ref[...]

    rows = lseg * nb
    nlev = int(math.log2(lseg))
    lb = lb_ref[...]
    i, j = _iotas(rows)
    hd = lambda t, h: t[:, h * HG_HEADDIM:(h + 1) * HG_HEADDIM]
    seg = lambda t, s: t[s * lseg:(s + 1) * lseg]
    cat = lambda parts: parts[0] if nb == 1 else jnp.concatenate(parts, axis=0)

    SMALL = 8
    lower, upper, pair = {}, {}, []
    for l in range(nlev + 1):
        bi = i >> l
        bj = j >> l
        if (1 << l) < SMALL or l == nlev:
            lower[l] = ((bi == bj) & (j <= i)).astype(BF16)
            upper[l] = ((bi == bj) & (j > i)).astype(BF16)
        pair.append((bi == bj + 1) & ((bi & 1) == 1))
    diag_mask = i == j
    col_seg = [(j[0:1, :] >> nlev) == s for s in range(nb)]
    rows_of = lambda t, h: t[h * HG_HEADDIM:(h + 1) * HG_HEADDIM]

    def block_row(t, size, which):
        t3 = t.reshape(rows // size, size, t.shape[1])
        return jnp.broadcast_to(t3[:, which:which + 1, :], t3.shape).reshape(t.shape)

    def problem(p):
        x = _load_rows(x_ref, p, nb, x_dense)
        states = [[s_ref[p * nb + s, h] for s in range(nb)] for h in range(HG_HEADS)]
        q = _silu(x[:, 0:HG_WIDTH])
        f = lb + (1.0 - lb) * _sigmoid(x[:, HG_WIDTH:2 * HG_WIDTH])
        logf = jnp.log(f)
        k = 1.0 - f
        v = x[:, 2 * HG_WIDTH:3 * HG_WIDTH].astype(BF16)
        gate = _sigmoid(x[:, 3 * HG_WIDTH:4 * HG_WIDTH])
        logf_parts = _split(logf, 3)
        msum = lambda m, n: sum(jnp.dot(m, part, preferred_element_type=F32) for part in logf_parts[:n])
        cum = msum(lower[nlev], 3)
        cum_ex = cum - logf

        att = [jnp.where(diag_mask, jnp.sum(hd(q, h) * hd(k, h), axis=-1, keepdims=True), 0.0)
               for h in range(HG_HEADS)]
        for l in range(nlev):
            size = 1 << l
            if l == 0:
                qe, ke = q * f, k
            elif size < SMALL:
                qe, ke = q * jnp.exp(msum(lower[l], 2)), k * jnp.exp(msum(upper[l], 2))
            else:
                qe = q * jnp.exp(cum - block_row(cum_ex, size, 0))
                ke = k * jnp.exp(block_row(cum, size, size - 1) - cum)
            qe = qe.astype(BF16)
            ke_t = ke.T.astype(BF16)
            for h in range(HG_HEADS):
                att[h] = att[h] + jnp.where(pair[l], _bdot(hd(qe, h), rows_of(ke_t, h)), 0.0)

        qin = (q * jnp.exp(cum)).astype(BF16)
        seg_tot = block_row(cum, lseg, lseg - 1)
        kend_t = (k * jnp.exp(seg_tot - cum)).T.astype(BF16)
        seg_tot_t = seg_tot.T
        outs, new_states = [], []
        for h in range(HG_HEADS):
            vh = hd(v, h)
            o = _bdot(att[h], vh) + cat([_bdot(seg(hd(qin, h), s), states[h][s]) for s in range(nb)])
            ns = []
            for s in range(nb):
                tot = jnp.broadcast_to(rows_of(seg_tot_t, h)[:, s * lseg:s * lseg + 1], (HG_HEADDIM, HG_HEADDIM))
                kend_s = rows_of(kend_t, h) if nb == 1 else jnp.where(col_seg[s], rows_of(kend_t, h), 0)
                ns.append(jnp.exp(tot) * states[h][s] + _bdot(kend_s, vh))
            new_states.append(ns)
            outs.append(_rms(o, nw_ref[...]) * hd(gate, h))
        _store_rows(y_ref, p, jnp.concatenate(outs, axis=1), nb, lseg)
        for h in range(HG_HEADS):
            for s in range(nb):
                s_ref[p * nb + s, h] = new_states[h][s]

    _for_each_problem(sub, nb, problem)


def _hgrn(x, s0, lb, norm_w, stacked, layer, bsz, seqlen, lseg, nb, sub):
    return _mixer_call(_hg_kernel, "hgrn_mixer", (x,), s0, (lb, norm_w),
                       (_const_in((1, HG_WIDTH)), _const_in((1, HG_HEADDIM))), stacked, layer,
                       HG_WIDTH, (4 * HG_WIDTH,), bsz, seqlen, lseg, nb, sub)


def _rw_kernel(x_ref, s0_ref, w0_ref, wup_ref, a0_ref, aup_ref, gup_ref, kk_ref, ka_ref, rk_ref, lnw_ref, lnb_ref,
               y_ref, s_ref, x_dense, *, lseg, nb, sub):
    _init_states_transposed(s_ref, s0_ref, RW_HEADS, sub)

    R = lseg * nb
    HD, PW, NH, W = RW_HEADDIM, 2 * RW_HEADDIM, RW_HEADS, RW_WIDTH
    assert R == PW
    nlev = int(math.log2(lseg))
    i, j = _iotas(R)
    same = (i >> nlev) == (j >> nlev)
    incl = same & (j <= i)
    strict = same & (j < i)
    m_incl = incl.astype(BF16)
    m_same = same.astype(BF16)
    eye = (i == j).astype(F32)
    pair = [((i >> l) == (j >> l) + 1) & (((i >> l) & 1) == 1) for l in range(nlev)]
    m_head = ((i >> 6) == (j >> 6)).astype(BF16)
    col_seg = [(j[0:1, :] >> nlev) == s for s in range(nb)]
    seg_cols = lambda t, s: t if nb == 1 else jnp.where(col_seg[s], t, 0)
    first_col = jnp.concatenate([(i == s * lseg).astype(BF16) for s in range(nb)], axis=1)
    hd = lambda t, h: t[:, h * HD:(h + 1) * HD]
    rows_of = lambda t, h: t[h * HD:(h + 1) * HD]
    seg = lambda t, s: t[s * lseg:(s + 1) * lseg]
    cat = lambda parts: parts[0] if nb == 1 else jnp.concatenate(parts, axis=0)

    def head_sum(t):
        return jnp.concatenate([_dot_mask(t[:, q * PW:(q + 1) * PW], m_head, 2) for q in range(W // PW)], axis=1)

    def problem(p):
        x = _load_rows(x_ref, p, nb, x_dense)
        states = [[s_ref[p * nb + s, h] for s in range(nb)] for h in range(NH)]
        r, k, v = x[:, 0:W], x[:, W:2 * W], x[:, 2 * W:3 * W]
        wd, ad, gd = x[:, 3 * W:3 * W + 64], x[:, 3 * W + 64:3 * W + 128], x[:, 3 * W + 128:3 * W + 256]
        w = -jax.nn.softplus(-(w0_ref[...] + _bdot(jnp.tanh(wd), wup_ref[...]))) - 0.5
        logw = -jnp.exp(w)
        ag = _sigmoid(a0_ref[...] + _bdot(ad, aup_ref[...]))
        g = _bdot(_sigmoid(gd), gup_ref[...])
        logw_parts = _split(logw, 2)
        rsum = lambda m: (jnp.dot(m, logw_parts[0], preferred_element_type=F32)
                          + jnp.dot(m, logw_parts[1], preferred_element_type=F32))
        b_in = rsum(m_incl)
        b_tot = rsum(m_same)
        e_in = jnp.exp(b_in)
        e_ex = jnp.exp(b_in - logw)

        kk = k * kk_ref[...]
        kk = kk * lax.rsqrt(jnp.maximum(head_sum(kk * kk), 1e-24))
        k2 = k * (1.0 + (ag - 1.0) * ka_ref[...])
        kb = kk * ag
        a_t = (-kk * e_ex).astype(BF16)
        r_t = (r * e_in).astype(BF16)
        vb = v.astype(BF16)
        bonus = head_sum(r * k2 * rk_ref[...]) * v

        b_in_t = b_in.T
        b_tot_t = b_tot.T
        e_neg_t = jnp.exp(-b_in_t)
        e_end_t = jnp.exp(b_tot_t - b_in_t)
        kb_t = kb.T
        k2_t = k2.T
        b_n = (kb_t * e_neg_t).astype(BF16)
        k_n = (k2_t * e_neg_t).astype(BF16)
        b_e = (kb_t * e_end_t).astype(BF16)
        k_e = (k2_t * e_end_t).astype(BF16)
        tot_parts = _split(b_tot_t, 2)

        keys = [jnp.concatenate([rows_of(b_n, h), rows_of(k_n, h)], axis=1) for h in range(NH)]
        am = [_bdot(hd(a_t, h), keys[h]) for h in range(NH)]
        rm = [_bdot(hd(r_t, h), keys[h]) for h in range(NH)]
        a_ab = [jnp.where(strict, am[h][:, :R], 0.0) for h in range(NH)]
        a_ak = [jnp.where(strict, am[h][:, R:], 0.0) for h in range(NH)]
        a_rb = [jnp.where(incl, rm[h][:, :R], 0.0) for h in range(NH)]
        a_rk = [jnp.where(incl, rm[h][:, R:], 0.0) for h in range(NH)]
        zmat = [cat([_bdot(seg(hd(a_t, h), s), states[h][s]) for s in range(nb)]) + _bdot(a_ak[h], hd(vb, h))
                for h in range(NH)]
        y0 = [cat([_bdot(seg(hd(r_t, h), s), states[h][s]) for s in range(nb)]) + _bdot(a_rk[h], hd(vb, h))
              for h in range(NH)]
        inv = [eye + jnp.where(pair[0], a_ab[h], 0.0) for h in range(NH)]
        for l in range(1, nlev):
            tmp = [_bdot(jnp.where(pair[l], a_ab[h], 0.0), inv[h]) for h in range(NH)]
            inv = [inv[h] + _bdot(inv[h], tmp[h]) for h in range(NH)]
        sa = [_bdot(inv[h], zmat[h]) for h in range(NH)]
        ys = [y0[h] + _bdot(a_rb[h], sa[h]) for h in range(NH)]
        e_tot = [jnp.exp(jnp.dot(rows_of(tot_parts[0], h), first_col, preferred_element_type=F32)
                         + jnp.dot(rows_of(tot_parts[1], h), first_col, preferred_element_type=F32))
                 for h in range(NH)]
        new_states = [[states[h][s] * e_tot[h][:, s * PW:s * PW + HD]
                       + _bdot(seg_cols(rows_of(b_e, h), s), sa[h]) + _bdot(seg_cols(rows_of(k_e, h), s), hd(vb, h))
                       for s in range(nb)] for h in range(NH)]
        y = jnp.concatenate(ys, axis=1)
        mu = head_sum(y) * (1.0 / RW_HEADDIM)
        yc = y - mu
        var = head_sum(yc * yc) * (1.0 / RW_HEADDIM)
        y = yc * lax.rsqrt(var + RW_LN_EPS) * lnw_ref[...] + lnb_ref[...] + bonus
        _store_rows(y_ref, p, y * g, nb, lseg)
        for h in range(NH):
            for s in range(nb):
                s_ref[p * nb + s, h] = new_states[h][s]

    _for_each_problem(sub, nb, problem)
    _finish_states_transposed(s_ref, RW_HEADS, sub)


def _rwkv(x, s0, p, stacked, layer, bsz, seqlen, lseg, nb, sub):
    vec = _const_in((1, RW_WIDTH))
    return _mixer_call(_rw_kernel, "rwkv_mixer", (x,), s0,
                       (p['w0'], p['w_up'], p['a0'], p['a_up'], p['g_up'], p['k_k'], p['k_a'], p['r_k'], p['ln_w'],
                        p['ln_b']),
                       (vec, _const_in((64, RW_WIDTH)), vec, _const_in((64, RW_WIDTH)), _const_in((128, RW_WIDTH)),
                        vec, vec, vec, vec, vec), stacked, layer,
                       RW_WIDTH, (RW_PROJ,), bsz, seqlen, lseg, nb, sub)


def _merge_kernel(x_ref, yssd_ref, ys5_ref, yhg_ref, yrw_ref, nw_ref, wm_ref, bm_ref,
                  wssd_ref, ws5_ref, whg_ref, wrw_ref, wout_ref, o_ref):
    x = x_ref[...]
    xn = _rms(x, nw_ref[...]).astype(BF16)
    merged = None
    for b, (y_ref, w_ref) in enumerate(((yssd_ref, wssd_ref), (ys5_ref, ws5_ref), (yhg_ref, whg_ref), (yrw_ref, wrw_ref))):
        cs = slice(b * D_MODEL, (b + 1) * D_MODEL)
        gate = _sigmoid(jnp.dot(xn, wm_ref[:, cs], preferred_element_type=F32) + bm_ref[:, cs])
        t = gate * jnp.dot(y_ref[...].astype(BF16), w_ref[...], preferred_element_type=F32)
        merged = t if merged is None else merged + t
    o_ref[...] = x + jnp.dot(merged.astype(BF16), wout_ref[...], preferred_element_type=F32)


def _merge(x, y_ssd, y_s5, y_hg, y_rw, tb, norm_w, w_merge, b_merge, w_ssd, w_s5, w_hg, w_rw, w_out):
    rows = x.shape[0]
    row = lambda w: pl.BlockSpec((tb, w), lambda i: (i, 0))
    return pl.pallas_call(
        _merge_kernel, grid=(rows // tb,),
        in_specs=[row(D_MODEL), row(1024), row(S5_WIDTH), row(HG_WIDTH), row(RW_WIDTH),
                  _const_in((1, D_MODEL)), _const_in((D_MODEL, 4 * D_MODEL)), _const_in((1, 4 * D_MODEL)),
                  _const_in((1024, D_MODEL)), _const_in((S5_WIDTH, D_MODEL)), _const_in((HG_WIDTH, D_MODEL)),
                  _const_in((RW_WIDTH, D_MODEL)), _const_in((D_MODEL, D_MODEL))],
        out_specs=row(D_MODEL), out_shape=jax.ShapeDtypeStruct((rows, D_MODEL), F32),
        compiler_params=_params(("arbitrary",)), name="merge",
    )(x, y_ssd, y_s5, y_hg, y_rw, norm_w, w_merge, b_merge, w_ssd, w_s5, w_hg, w_rw, w_out)


def _ffn_kernel(x_ref, nw_ref, wup_ref, cw_ref, cb_ref, wdn_ref, conv0_ref, fnw_ref, o_ref, convst_ref, full_ref,
                *, bsz, tb, final_norm):
    lookback = (FFN_CONV - 1) * bsz

    @pl.when(pl.program_id(0) == 0)
    def _():
        full_ref[0:lookback, :] = conv0_ref[...]

    x = x_ref[...]
    xn = _rms(x, nw_ref[...]).astype(BF16)
    full_ref[lookback:lookback + tb, :] = jnp.dot(xn, wup_ref[...], preferred_element_type=F32)
    acc = cb_ref[...] + full_ref[0:tb, :] * cw_ref[0:1, :]
    for j in range(1, FFN_CONV):
        acc = acc + full_ref[j * bsz:j * bsz + tb, :] * cw_ref[j:j + 1, :]
    _shift_rows_down(full_ref, tb, lookback)
    convst_ref[...] = full_ref[0:lookback, :]
    hidden = jax.nn.gelu(acc[:, 0:D_FF]) * acc[:, D_FF:2 * D_FF]
    out = x + jnp.dot(hidden.astype(BF16), wdn_ref[...], preferred_element_type=F32)
    if final_norm:
        out = _rms(out, fnw_ref[...])
    o_ref[...] = out


def _ffn(x, bsz, tb, norm_w, w_up, conv_w, conv_b, w_down, conv0, final_w, final_norm):
    rows = x.shape[0]
    lookback = (FFN_CONV - 1) * bsz
    assert rows % tb == 0 and tb % bsz == 0
    row = pl.BlockSpec((tb, D_MODEL), lambda i: (i, 0))
    return pl.pallas_call(
        functools.partial(_ffn_kernel, bsz=bsz, tb=tb, final_norm=final_norm),
        grid=(rows // tb,),
        in_specs=[row, _const_in((1, D_MODEL)), _const_in((D_MODEL, 2 * D_FF)), _const_in((FFN_CONV, 2 * D_FF)),
                  _const_in((1, 2 * D_FF)), _const_in((D_FF, D_MODEL)), _const_in((lookback, 2 * D_FF)),
                  _const_in((1, D_MODEL))],
        out_specs=(row, _const_spec((lookback, 2 * D_FF))),
        out_shape=(jax.ShapeDtypeStruct((rows, D_MODEL), F32), jax.ShapeDtypeStruct((lookback, 2 * D_FF), F32)),
        scratch_shapes=[pltpu.VMEM((lookback + tb, 2 * D_FF), F32)],
        compiler_params=_params(("arbitrary",)), name="conv_ffn",
    )(x, norm_w, w_up, conv_w, conv_b, w_down, conv0, final_w)


def _pad_lanes(v, width=DT_PAD):
    return jnp.pad(v, (0, width - v.shape[0])).reshape(1, width)


def _layer_params(l, P, lb_all):
    p = {n: a[l] for n, a in P.items()}
    row = lambda a: a.reshape(1, -1)
    ab_re, ab_im, s5_win, s5_wcre, s5_wcim = _s5_prep(p['s5_log_dt'], p['s5_a_re'], p['s5_a_im'], p['s5_b_re'],
                                                      p['s5_b_im'], p['s5_c_re'], p['s5_c_im'])
    q = dict(
        norm1_w=row(p['norm1_w']), w_in=_pack_w_in(p['w_in']),
        ssd_conv_w=p['ssd_conv_w'], ssd_conv_b=row(p['ssd_conv_b']), ssd_dt_bias=_pad_lanes(p['ssd_dt_bias']),
        ssd_a_log=_pad_lanes(p['ssd_a_log']), ssd_d=_pad_lanes(p['ssd_d']), ssd_norm_w=row(p['ssd_norm_w']),
        s5_win=s5_win, s5_wcre=s5_wcre, s5_wcim=s5_wcim,
        s5_ab_re=row(ab_re), s5_ab_im=row(ab_im), s5_d=row(p['s5_d']),
        s5_glu_w=p['s5_glu_w'].astype(BF16), s5_glu_b=row(p['s5_glu_b']),
        hg_lb=lb_all[l:l + 1], hg_norm_w=row(p['hg_norm_w']),
        rw_mu=row(p['rw_mu']),
        rw=dict(w0=row(p['rw_w0']), w_up=p['rw_w_up'].astype(BF16), a0=row(p['rw_a0']), a_up=p['rw_a_up'].astype(BF16),
                g_up=p['rw_g_up'].astype(BF16), k_k=row(p['rw_k_k']), k_a=row(p['rw_k_a']), r_k=row(p['rw_r_k']),
                ln_w=row(p['rw_ln_w']), ln_b=row(p['rw_ln_b'])),
        w_merge=p['w_merge'].astype(BF16), b_merge=row(p['b_merge']),
        w_br_ssd=p['w_br_ssd'].astype(BF16), w_br_s5=p['w_br_s5'].astype(BF16), w_br_hg=p['w_br_hg'].astype(BF16),
        w_br_rw=p['w_br_rw'].astype(BF16), w_out=p['w_out'].astype(BF16),
        norm2_w=row(p['norm2_w']), ffn_up=p['ffn_up'].astype(BF16), ffn_conv_w=p['ffn_conv_w'],
        ffn_conv_b=row(p['ffn_conv_b']), ffn_down=p['ffn_down'].astype(BF16),
    )
    return q


class _Group:
    def __init__(self, bsz, seqlen):
        self.bsz, self.seqlen = bsz, seqlen
        rows = bsz * seqlen
        self.tb = min(rows, max(256, bsz))
        self.tt = self.tb // bsz
        if seqlen >= 128:
            self.ssd, self.hg, self.rw = (128, 2, 8), (64, 2, 8), (64, 2, 8)
        else:
            self.ssd = self.hg = self.rw = (seqlen, 128 // seqlen, 128 // seqlen)


def _time_major(a):
    a = jnp.swapaxes(a, 0, 1)
    return a.reshape((a.shape[0] * a.shape[1],) + a.shape[2:])


def _batch_major(a, bsz):
    a = a.reshape((a.shape[0] // bsz, bsz) + a.shape[1:])
    return jnp.swapaxes(a, 0, 1)


def _trunk(x, states, layers, final_norm_w, grp):
    bsz, seqlen = grp.bsz, grp.seqlen
    xt = _time_major(x)
    small_states = []
    ssd_all = hg_all = rw_all = None
    for l, q in enumerate(layers):
        s_ssd, s_conv, s_s5r, s_s5i, s_hg, s_rw, s_shift, s_fconv = states[l]
        z, xbc, dt, u, hg, rwx, conv_new, shift_new = _in_proj(
            xt, bsz, grp.tb, q['norm1_w'], q['w_in'], q['ssd_conv_w'], q['ssd_conv_b'], q['ssd_dt_bias'], q['rw_mu'],
            _time_major(s_conv), s_shift)
        y_ssd, ssd_all = _ssd(z, xbc, dt, s_ssd, q['ssd_a_log'], q['ssd_d'], q['ssd_norm_w'], ssd_all, l,
                              bsz, seqlen, *grp.ssd)
        y_s5, hr_new, hi_new = _s5(u, bsz, grp.tt, q['s5_win'], q['s5_wcre'], q['s5_wcim'], q['s5_ab_re'], q['s5_ab_im'],
                                   q['s5_d'], q['s5_glu_w'], q['s5_glu_b'],
                                   s_s5r.reshape(bsz, S5_CH), s_s5i.reshape(bsz, S5_CH))
        y_hg, hg_all = _hgrn(hg, s_hg, q['hg_lb'], q['hg_norm_w'], hg_all, l, bsz, seqlen, *grp.hg)
        y_rw, rw_all = _rwkv(rwx, s_rw, q['rw'], rw_all, l, bsz, seqlen, *grp.rw)
        x1 = _merge(xt, y_ssd, y_s5, y_hg, y_rw, grp.tb, q['norm1_w'], q['w_merge'], q['b_merge'],
                    q['w_br_ssd'], q['w_br_s5'], q['w_br_hg'], q['w_br_rw'], q['w_out'])
        xt, fconv_new = _ffn(x1, bsz, grp.tb, q['norm2_w'], q['ffn_up'], q['ffn_conv_w'], q['ffn_conv_b'],
                             q['ffn_down'], _time_major(s_fconv), final_norm_w.reshape(1, D_MODEL), l == DEPTH - 1)
        small_states.append((_batch_major(conv_new, bsz),
                             hr_new.reshape(bsz, S5_GROUPS, S5_STATE), hi_new.reshape(bsz, S5_GROUPS, S5_STATE),
                             shift_new, _batch_major(fconv_new, bsz)))
    conv_all, s5r_all, s5i_all, shift_all, fconv_all = (
        jnp.stack([st[k] for st in small_states], axis=0) for k in range(5))
    return _batch_major(xt, bsz), (ssd_all, conv_all, s5r_all, s5i_all, hg_all, rw_all, shift_all, fconv_all)


def _zero_states(bsz):
    z = lambda *s: jnp.zeros((bsz,) + s, F32)
    return (z(SSD_HEADS, SSD_HEADDIM, SSD_D_STATE), z(SSD_CONV - 1, SSD_CONV_CH), z(S5_GROUPS, S5_STATE),
            z(S5_GROUPS, S5_STATE), z(HG_HEADS, HG_HEADDIM, HG_HEADDIM), z(RW_HEADS, RW_HEADDIM, RW_HEADDIM),
            z(RW_PROJ), z(FFN_CONV - 1, 2 * D_FF))


def kernel(x_prompt, x_sample, state_ssd, state_ssd_conv, state_s5_re, state_s5_im, state_hgrn, state_rwkv, state_rwkv_shift, state_ffn_conv, norm1_w, w_in, ssd_conv_w, ssd_conv_b, ssd_dt_bias, ssd_a_log, ssd_d, ssd_norm_w, s5_a_re, s5_a_im, s5_log_dt, s5_b_re, s5_b_im, s5_c_re, s5_c_im, s5_d, s5_glu_w, s5_glu_b, hg_lb_raw, hg_norm_w, rw_mu, rw_w0, rw_w_up, rw_a0, rw_a_up, rw_g_up, rw_k_k, rw_k_a, rw_r_k, rw_ln_w, rw_ln_b, w_br_ssd, w_br_s5, w_br_hg, w_br_rw, w_merge, b_merge, w_out, norm2_w, ffn_up, ffn_conv_w, ffn_conv_b, ffn_down, final_norm_w):
    P = dict(norm1_w=norm1_w, w_in=w_in, ssd_conv_w=ssd_conv_w, ssd_conv_b=ssd_conv_b, ssd_dt_bias=ssd_dt_bias,
             ssd_a_log=ssd_a_log, ssd_d=ssd_d, ssd_norm_w=ssd_norm_w, s5_a_re=s5_a_re, s5_a_im=s5_a_im,
             s5_log_dt=s5_log_dt, s5_b_re=s5_b_re, s5_b_im=s5_b_im, s5_c_re=s5_c_re, s5_c_im=s5_c_im, s5_d=s5_d,
             s5_glu_w=s5_glu_w, s5_glu_b=s5_glu_b, hg_norm_w=hg_norm_w, rw_mu=rw_mu, rw_w0=rw_w0, rw_w_up=rw_w_up,
             rw_a0=rw_a0, rw_a_up=rw_a_up, rw_g_up=rw_g_up,
             rw_k_k=rw_k_k.reshape(DEPTH, RW_WIDTH), rw_k_a=rw_k_a.reshape(DEPTH, RW_WIDTH),
             rw_r_k=rw_r_k.reshape(DEPTH, RW_WIDTH), rw_ln_w=rw_ln_w.reshape(DEPTH, RW_WIDTH),
             rw_ln_b=rw_ln_b.reshape(DEPTH, RW_WIDTH),
             w_br_ssd=w_br_ssd, w_br_s5=w_br_s5, w_br_hg=w_br_hg, w_br_rw=w_br_rw, w_merge=w_merge, b_merge=b_merge,
             w_out=w_out, norm2_w=norm2_w, ffn_up=ffn_up, ffn_conv_w=ffn_conv_w, ffn_conv_b=ffn_conv_b,
             ffn_down=ffn_down)
    lb_all = _lower_bounds(hg_lb_raw)
    layers = [_layer_params(l, P, lb_all) for l in range(DEPTH)]
    sample_states = (state_ssd, state_ssd_conv, state_s5_re, state_s5_im, state_hgrn, state_rwkv,
                     state_rwkv_shift, state_ffn_conv)
    sample_init = [tuple(s[l] for s in sample_states) for l in range(DEPTH)]
    prompt_init = [_zero_states(x_prompt.shape[0])] * DEPTH
    y_prompt, p_states = _trunk(x_prompt, prompt_init, layers, final_norm_w, _Group(*x_prompt.shape[:2]))
    y_sample, s_states = _trunk(x_sample, sample_init, layers, final_norm_w, _Group(*x_sample.shape[:2]))
    return (y_prompt, y_sample) + p_states + s_states
```
